```python
import math
import jax
import jax.numpy as jnp
from jax import lax
import numpy as np

D_MODEL = 2048
BATCH = 8
SEQ = 2048
DEPTH = 1
DEC_BATCH = 128
DEC_SEQ = 8
PAST_LEN = 2048
PAGE_SIZE = 128

A_GROUPS = 4
A_GROUP_DIM = 128
A_WIDTH = A_GROUPS * A_GROUP_DIM
CHUNK = 128
B_HEADS = 8
B_HEAD_DIM = 64
B_V_DIM = 2 * B_HEAD_DIM
B_QK_WIDTH = B_HEADS * 2 * B_HEAD_DIM
B_WIDTH = B_HEADS * B_V_DIM
Q_BLOCK = 128
ROPE_THETA = 10000.0
N_MEM = 256
C_HEADS = 4
C_HEAD_DIM = 128
C_WIDTH = C_HEADS * C_HEAD_DIM
N_BRANCH = 3
IN_WIDTH = 2 * A_WIDTH + 2 * B_QK_WIDTH + B_WIDTH + C_WIDTH + N_BRANCH * D_MODEL
N_EXPERTS = 256
TOP_K = 8
N_GROUPS = 8
TOPK_GROUPS = 4
D_EXPERT = 576
ROUTED_SCALE = 2.5
MOE_BLOCK = 64
DEEPNORM_ALPHA = (2.0 * DEPTH) ** 0.25
DEEPNORM_BETA = (8.0 * DEPTH) ** -0.25
LN_EPS = 1e-5

kernel_name = 'hybrid_gmlp_diffattn_memxattn_moe_step'


def lambda_init(layer):
    return 0.8 - 0.6 * math.exp(-0.3 * layer)


def layer_norm(x, g, b):
    xf = x.astype(jnp.float32)
    mu = jnp.mean(xf, -1, keepdims=True)
    var = jnp.mean(jnp.square(xf - mu), -1, keepdims=True)
    return ((xf - mu) * lax.rsqrt(var + LN_EPS) * g + b).astype(x.dtype)


def rms_norm(x, g):
    xf = x.astype(jnp.float32)
    return (xf * lax.rsqrt(jnp.mean(xf * xf, -1, keepdims=True) + LN_EPS) * g).astype(x.dtype)


def rope(x, pos):
    half = x.shape[-1] // 2
    inv = 1.0 / (ROPE_THETA ** (jnp.arange(half, dtype=jnp.float32) / half))
    ang = pos.astype(jnp.float32)[:, None] * inv[None, :]
    bshape = (pos.shape[0],) + (1,) * (x.ndim - 3) + (half,)
    cos = jnp.cos(ang).reshape(bshape)
    sin = jnp.sin(ang).reshape(bshape)
    x1 = x[..., :half].astype(jnp.float32)
    x2 = x[..., half:].astype(jnp.float32)
    return jnp.concatenate([x1 * cos - x2 * sin, x2 * cos + x1 * sin], -1).astype(x.dtype)


def in_projection(x, pos, w_in, a_ln_g, a_ln_b):
    b, t = x.shape[:2]
    h = x @ w_in
    idx = np.cumsum([A_WIDTH, A_WIDTH, B_QK_WIDTH, B_QK_WIDTH, B_WIDTH, C_WIDTH]).tolist()
    a_u, a_v, q, k, v, cq, gates = jnp.split(h, idx, axis=-1)
    u = jax.nn.gelu(a_u)
    vn = layer_norm(jax.nn.gelu(a_v), a_ln_g, a_ln_b)
    q = rope(q.reshape(b, t, B_HEADS, 2, B_HEAD_DIM), pos)
    k = rope(k.reshape(b, t, B_HEADS, 2, B_HEAD_DIM), pos)
    v = v.reshape(b, t, B_HEADS, B_V_DIM)
    cq = cq.reshape(b, t, C_HEADS, C_HEAD_DIM)
    return u, vn, q, k, v, cq, gates


def spatial_gating(u, vn, ws, bs):
    b, t, _ = u.shape
    n = ws.shape[-1]
    shp = (b, t // n, n, A_GROUPS, A_GROUP_DIM)
    causal = jnp.tril(jnp.ones((n, n), dtype=bool))
    w = jnp.where(causal, ws, jnp.zeros((), ws.dtype))
    s = jnp.einsum('gts,bcsgk->bctgk', w, vn.reshape(shp)) + bs.T[None, None, :, :, None]
    return (u.reshape(shp) * s).reshape(b, t, A_WIDTH)


def diff_lambda(lq1, lk1, lq2, lk2, layer):
    f32 = jnp.float32
    return (jnp.exp(jnp.sum(lq1.astype(f32) * lk1.astype(f32)))
            - jnp.exp(jnp.sum(lq2.astype(f32) * lk2.astype(f32))) + lambda_init(layer))


def diff_attention(q, q_pos, segments, lam):
    scale = B_HEAD_DIM ** -0.5
    scores = []
    for k, _, k_pos in segments:
        s = jnp.einsum('bqhcd,bkhcd->bhcqk', q, k, preferred_element_type=jnp.float32) * scale
        scores.append(jnp.where(k_pos[None, :] <= q_pos[:, None], s, -jnp.inf))
    p = jax.nn.softmax(jnp.concatenate(scores, -1), axis=-1)
    p = p[:, :, 0] - lam * p[:, :, 1]
    outs = []
    start = 0
    for _, v, _ in segments:
        n = v.shape[1]
        outs.append(jnp.einsum('bhqk,bkhe->bqhe', p[..., start:start + n].astype(v.dtype), v))
        start += n
    return sum(outs[1:], outs[0])


def diff_attention_prompt(q, k, v, pos, lam):
    b, t = q.shape[:2]
    nb = t // Q_BLOCK
    q_blocks = jnp.moveaxis(q.reshape((b, nb, Q_BLOCK) + q.shape[2:]), 1, 0)
    pos_blocks = pos.reshape(nb, Q_BLOCK)
    o = lax.map(lambda qp: diff_attention(qp[0], qp[1], ((k, v, pos),), lam), (q_blocks, pos_blocks))
    return jnp.moveaxis(o, 0, 1).reshape(b, t, B_HEADS, B_V_DIM)


def memory_kv(mem, w):
    return (mem @ w).reshape(mem.shape[0], mem.shape[1], C_HEADS, C_HEAD_DIM)


def memory_attention(cq, mk, mv):
    s = jnp.einsum('bqhd,bmhd->bhqm', cq, mk, preferred_element_type=jnp.float32) * (C_HEAD_DIM ** -0.5)
    p = jax.nn.softmax(s, axis=-1)
    return jnp.einsum('bhqm,bmhd->bqhd', p.astype(mv.dtype), mv)


def merge_branches(o_a, o_b, o_c, gates, subln_g, w_pa, w_pb, w_pc, w_o, layer):
    b, t = o_a.shape[:2]
    o_b = (rms_norm(o_b, subln_g) * (1.0 - lambda_init(layer))).reshape(b, t, B_WIDTH)
    g = jax.nn.sigmoid(gates).reshape(b, t, N_BRANCH, D_MODEL)
    m = (g[:, :, 0] * (o_a @ w_pa) + g[:, :, 1] * (o_b @ w_pb)
         + g[:, :, 2] * (o_c.reshape(b, t, C_WIDTH) @ w_pc))
    return m @ w_o


def moe_ffn(x, layer, w_router, router_bias, w_gate, w_up, w_down, ws_gate, ws_up, ws_down):
    shp = x.shape
    xf = x.reshape(-1, D_MODEL)
    t = xf.shape[0]
    s = jax.nn.sigmoid((xf @ w_router).astype(jnp.float32))
    sb = s + router_bias.astype(jnp.float32)
    grp = lax.top_k(sb.reshape(t, N_GROUPS, N_EXPERTS // N_GROUPS), 2)[0].sum(-1)
    top_g = lax.top_k(grp, TOPK_GROUPS)[1]
    gmask = jnp.any(top_g[:, :, None] == jnp.arange(N_GROUPS)[None, None, :], axis=1)
    emask = jnp.repeat(gmask, N_EXPERTS // N_GROUPS, axis=1)
    idx = lax.top_k(jnp.where(emask, sb, -jnp.inf), TOP_K)[1]
    wt = jnp.take_along_axis(s, idx, axis=1)
    wt = wt / jnp.sum(wt, -1, keepdims=True) * ROUTED_SCALE
    tk = t * TOP_K
    flat_e = idx.reshape(-1)
    order = jnp.argsort(flat_e)
    se = flat_e[order]
    counts = jnp.bincount(flat_e, length=N_EXPERTS)
    pcounts = (counts + MOE_BLOCK - 1) // MOE_BLOCK * MOE_BLOCK
    pend = jnp.cumsum(pcounts)
    dest = (pend - pcounts)[se] + jnp.arange(tk) - (jnp.cumsum(counts) - counts)[se]
    n_blocks = -(-tk // MOE_BLOCK) + N_EXPERTS
    n_rows = n_blocks * MOE_BLOCK
    row_tok = jnp.full((n_rows,), t, jnp.int32).at[dest].set((order // TOP_K).astype(jnp.int32))
    row_w = jnp.zeros((n_rows,), jnp.float32).at[dest].set(wt.reshape(-1)[order])
    blk_e = jnp.minimum(jnp.searchsorted(pend, jnp.arange(n_blocks) * MOE_BLOCK, side='right'), N_EXPERTS - 1)
    x_pad = jnp.concatenate([xf, jnp.zeros((1, D_MODEL), xf.dtype)], 0)

    def expert_block(args):
        rows, e = args
        xb = x_pad[rows]
        hb = jax.nn.silu(xb @ w_gate[layer, e]) * (xb @ w_up[layer, e])
        return hb @ w_down[layer, e]

    out = lax.map(expert_block, (row_tok.reshape(n_blocks, MOE_BLOCK), blk_e))
    out = out.reshape(n_rows, D_MODEL)
    routed = jax.ops.segment_sum(out * row_w[:, None].astype(out.dtype), row_tok, num_segments=t + 1)[:t]
    shared = (jax.nn.silu(xf @ ws_gate) * (xf @ ws_up)) @ ws_down
    return (routed.astype(x.dtype) + shared).reshape(shp)


def setup_inputs(seed: int = 0) -> dict:
    key = jax.random.key(seed)
    ks = iter(jax.random.split(key, 48))

    def nrm(shape, scale):
        return jax.random.normal(next(ks), shape, jnp.float32) * scale

    L = DEPTH
    n_pages = PAST_LEN // PAGE_SIZE
    n_used = DEC_BATCH * n_pages
    n_phys = n_used + max(1, n_used // 4)
    beta = DEEPNORM_BETA
    page_table = jax.random.permutation(next(ks), n_phys)[:n_used].reshape(DEC_BATCH, n_pages).astype(jnp.int32)
    return {
        'x_prompt': nrm((BATCH, SEQ, D_MODEL), 1.0),
        'x_sample': nrm((DEC_BATCH, DEC_SEQ, D_MODEL), 1.0),
        'mem_prompt': nrm((BATCH, N_MEM, D_MODEL), 1.0),
        'cache_k': nrm((L, n_phys, PAGE_SIZE, B_HEADS, 2, B_HEAD_DIM), 1.0),
        'cache_v': nrm((L, n_phys, PAGE_SIZE, B_HEADS, B_V_DIM), 1.0),
        'page_table': page_table,
        'cache_mem_k': nrm((L, DEC_BATCH, N_MEM, C_HEADS, C_HEAD_DIM), 1.0),
        'cache_mem_v': nrm((L, DEC_BATCH, N_MEM, C_HEADS, C_HEAD_DIM), 1.0),
        'w_in': nrm((L, D_MODEL, IN_WIDTH), D_MODEL ** -0.5),
        'a_ln_g': 1.0 + nrm((L, A_WIDTH), 0.02),
        'a_ln_b': nrm((L, A_WIDTH), 0.02),
        'a_ws': nrm((L, A_GROUPS, CHUNK, CHUNK), CHUNK ** -0.5),
        'a_bs': 1.0 + nrm((L, A_GROUPS, CHUNK), 0.02),
        'lam_q1': nrm((L, B_HEAD_DIM), 0.1),
        'lam_k1': nrm((L, B_HEAD_DIM), 0.1),
        'lam_q2': nrm((L, B_HEAD_DIM), 0.1),
        'lam_k2': nrm((L, B_HEAD_DIM), 0.1),
        'b_subln_g': 1.0 + nrm((L, B_V_DIM), 0.02),
        'w_mk': nrm((L, D_MODEL, C_WIDTH), D_MODEL ** -0.5),
        'w_mv': nrm((L, D_MODEL, C_WIDTH), D_MODEL ** -0.5 * beta),
        'w_pa': nrm((L, A_WIDTH, D_MODEL), A_WIDTH ** -0.5 * beta),
        'w_pb': nrm((L, B_WIDTH, D_MODEL), B_WIDTH ** -0.5 * beta),
        'w_pc': nrm((L, C_WIDTH, D_MODEL), C_WIDTH ** -0.5 * beta),
        'w_o': nrm((L, D_MODEL, D_MODEL), D_MODEL ** -0.5 * beta),
        'ln1_g': 1.0 + nrm((L, D_MODEL), 0.02),
        'ln1_b': nrm((L, D_MODEL), 0.02),
        'w_router': nrm((L, D_MODEL, N_EXPERTS), D_MODEL ** -0.5),
        'router_bias': nrm((L, N_EXPERTS), 0.01),
        'w_gate': nrm((L, N_EXPERTS, D_MODEL, D_EXPERT), D_MODEL ** -0.5),
        'w_up': nrm((L, N_EXPERTS, D_MODEL, D_EXPERT), D_MODEL ** -0.5),
        'w_down': nrm((L, N_EXPERTS, D_EXPERT, D_MODEL), D_EXPERT ** -0.5 * beta),
        'ws_gate': nrm((L, D_MODEL, D_EXPERT), D_MODEL ** -0.5),
        'ws_up': nrm((L, D_MODEL, D_EXPERT), D_MODEL ** -0.5),
        'ws_down': nrm((L, D_EXPERT, D_MODEL), D_EXPERT ** -0.5 * beta),
        'ln2_g': 1.0 + nrm((L, D_MODEL), 0.02),
        'ln2_b': nrm((L, D_MODEL), 0.02),
    }


def reference(x_prompt, x_sample, mem_prompt, cache_k, cache_v, page_table, cache_mem_k, cache_mem_v,
              w_in, a_ln_g, a_ln_b, a_ws, a_bs, lam_q1, lam_k1, lam_q2, lam_k2, b_subln_g,
              w_mk, w_mv, w_pa, w_pb, w_pc, w_o, ln1_g, ln1_b,
              w_router, router_bias, w_gate, w_up, w_down, ws_gate, ws_up, ws_down, ln2_g, ln2_b):
    pos_p = jnp.arange(SEQ, dtype=jnp.int32)
    pos_s = PAST_LEN + jnp.arange(DEC_SEQ, dtype=jnp.int32)
    pos_past = jnp.arange(PAST_LEN, dtype=jnp.int32)
    n_seq = page_table.shape[0]
    hp, hs = x_prompt, x_sample
    k_p, v_p, mk_p, mv_p, k_s, v_s, av_s = [], [], [], [], [], [], []
    for l in range(DEPTH):
        lam = diff_lambda(lam_q1[l], lam_k1[l], lam_q2[l], lam_k2[l], l)
        u, vn, q, k, v, cq, gates = in_projection(hp, pos_p, w_in[l], a_ln_g[l], a_ln_b[l])
        o_a = spatial_gating(u, vn, a_ws[l], a_bs[l])
        o_b = diff_attention_prompt(q, k, v, pos_p, lam)
        mk = memory_kv(mem_prompt, w_mk[l])
        mv = memory_kv(mem_prompt, w_mv[l])
        o_c = memory_attention(cq, mk, mv)
        mix = merge_branches(o_a, o_b, o_c, gates, b_subln_g[l], w_pa[l], w_pb[l], w_pc[l], w_o[l], l)
        hp = layer_norm(DEEPNORM_ALPHA * hp + mix, ln1_g[l], ln1_b[l])
        ffn = moe_ffn(hp, l, w_router[l], router_bias[l], w_gate, w_up, w_down, ws_gate[l], ws_up[l], ws_down[l])
        hp = layer_norm(DEEPNORM_ALPHA * hp + ffn, ln2_g[l], ln2_b[l])
        k_p.append(k)
        v_p.append(v)
        mk_p.append(mk)
        mv_p.append(mv)
        u, vn, q, k, v, cq, gates = in_projection(hs, pos_s, w_in[l], a_ln_g[l], a_ln_b[l])
        o_a = spatial_gating(u, vn, a_ws[l, :, :DEC_SEQ, :DEC_SEQ], a_bs[l, :, :DEC_SEQ])
        k_past = cache_k[l, page_table].reshape(n_seq, PAST_LEN, B_HEADS, 2, B_HEAD_DIM)
        v_past = cache_v[l, page_table].reshape(n_seq, PAST_LEN, B_HEADS, B_V_DIM)
        o_b = diff_attention(q, pos_s, ((k_past, v_past, pos_past), (k, v, pos_s)), lam)
        o_c = memory_attention(cq, cache_mem_k[l], cache_mem_v[l])
        mix = merge_branches(o_a, o_b, o_c, gates, b_subln_g[l], w_pa[l], w_pb[l], w_pc[l], w_o[l], l)
        hs = layer_norm(DEEPNORM_ALPHA * hs + mix, ln1_g[l], ln1_b[l])
        ffn = moe_ffn(hs, l, w_router[l], router_bias[l], w_gate, w_up, w_down, ws_gate[l], ws_up[l], ws_down[l])
        hs = layer_norm(DEEPNORM_ALPHA * hs + ffn, ln2_g[l], ln2_b[l])
        k_s.append(k)
        v_s.append(v)
        av_s.append(vn)
    return (hp, hs, jnp.stack(k_p), jnp.stack(v_p), jnp.stack(mk_p), jnp.stack(mv_p),
            jnp.stack(k_s), jnp.stack(v_s), jnp.stack(av_s))
```

```python
import functools
import math

import jax
import jax.numpy as jnp
from jax import lax
from jax.experimental import pallas as pl
from jax.experimental.pallas import tpu as pltpu

F32 = jnp.float32
BF16 = jnp.bfloat16

LN_EPS = 1e-5
ROPE_THETA = 10000.0
LANES = 128
A_GROUP_DIM = 128
A_GROUPS = 4
CHUNK = 128
B_HEAD_DIM = 64
B_V_DIM = 128
C_HEAD_DIM = 128
N_GROUPS = 8
TOPK_GROUPS = 4
TOP_K = 8
ROUTED_SCALE = 2.5
MOE_ROWS = 256
VMEM_LIMIT = 56 * 1024 * 1024


def _lambda_init(layer):
    return 0.8 - 0.6 * math.exp(-0.3 * layer)


def _params(*sem):
    return pltpu.CompilerParams(dimension_semantics=sem, vmem_limit_bytes=VMEM_LIMIT)


def _tile(n, pref):
    t = min(n, pref)
    while n % t:
        t //= 2
    return t


def _dot(a, b):
    return jnp.dot(a, b, preferred_element_type=F32)


def _dot_t(a, b):
    return lax.dot_general(a, b, (((1,), (1,)), ((), ())), preferred_element_type=F32)


def _mm_kernel(*refs, epilogue, scale):
    if epilogue == "rope":
        x_ref, w_ref, cos_ref, sin_ref, o_ref = refs
    else:
        x_ref, w_ref, o_ref = refs
    acc = _dot(x_ref[...], w_ref[...])
    if epilogue == "rope":
        n = acc.shape[1] // LANES
        cos = jnp.concatenate([cos_ref[...]] * n, axis=1)
        sin = jnp.concatenate([sin_ref[...]] * n, axis=1)
        lane = lax.broadcasted_iota(jnp.int32, acc.shape, 1)
        first = (lane % B_HEAD_DIM) < (B_HEAD_DIM // 2)
        half = B_HEAD_DIM // 2
        swapped = jnp.where(first, pltpu.roll(acc, acc.shape[1] - half, 1), pltpu.roll(acc, half, 1))
        acc = acc * cos + swapped * sin
    elif epilogue == "sigmoid":
        acc = jax.nn.sigmoid(acc)
    if scale != 1.0:
        acc = acc * scale
    o_ref[...] = acc.astype(o_ref.dtype)


def _mm(x, w, out_dtype, *, epilogue="none", scale=1.0, rope=None, tm=512, tn=512):
    m, k = x.shape
    n = w.shape[1]
    tm, tn = _tile(m, tm), _tile(n, tn)
    in_specs = [pl.BlockSpec((tm, k), lambda i, j: (i, 0)),
                pl.BlockSpec((k, tn), lambda i, j: (0, j))]
    args = [x, w]
    if epilogue == "rope":
        in_specs += [pl.BlockSpec((tm, LANES), lambda i, j: (i, 0))] * 2
        args += list(rope)
    return pl.pallas_call(
        functools.partial(_mm_kernel, epilogue=epilogue, scale=scale),
        out_shape=jax.ShapeDtypeStruct((m, n), out_dtype),
        grid=(m // tm, n // tn),
        in_specs=in_specs,
        out_specs=pl.BlockSpec((tm, tn), lambda i, j: (i, j)),
        compiler_params=_params("parallel", "arbitrary"),
        name="proj_" + epilogue,
    )(*args)


def _proj_a_kernel(x_ref, w_ref, lng_ref, lnb_ref, ws_ref, bs_ref, oa_ref, vn_ref):
    h = _dot(x_ref[...], w_ref[...])
    aw = h.shape[1] // 2
    u = jax.nn.gelu(h[:, :aw])
    g = jax.nn.gelu(h[:, aw:])
    mu = jnp.mean(g, axis=-1, keepdims=True)
    var = jnp.mean(jnp.square(g - mu), axis=-1, keepdims=True)
    vn = (g - mu) * lax.rsqrt(var + LN_EPS) * lng_ref[...] + lnb_ref[...]
    vn_ref[...] = vn
    vnb = vn.astype(BF16)
    row = lax.broadcasted_iota(jnp.int32, (CHUNK, CHUNK), 0)
    col = lax.broadcasted_iota(jnp.int32, (CHUNK, CHUNK), 1)
    causal = col <= row
    for gi in range(A_GROUPS):
        wsg = jnp.where(causal, ws_ref[0, gi], jnp.zeros((), BF16))
        cols = slice(gi * A_GROUP_DIM, (gi + 1) * A_GROUP_DIM)
        for c in range(h.shape[0] // CHUNK):
            rows = slice(c * CHUNK, (c + 1) * CHUNK)
            s = _dot(wsg, vnb[rows, cols]) + bs_ref[0, gi]
            oa_ref[rows, cols] = (u[rows, cols] * s).astype(oa_ref.dtype)


def _proj_a(x, w, lng, lnb, ws2, bs2, n_prompt, *, tm=512):
    m, k = x.shape
    n = w.shape[1]
    tm = _tile(math.gcd(m, n_prompt), tm)
    npt = n_prompt // tm
    sel = lambda i: (jnp.where(i >= npt, 1, 0), 0, 0, 0)
    return pl.pallas_call(
        _proj_a_kernel,
        out_shape=(jax.ShapeDtypeStruct((m, n // 2), BF16), jax.ShapeDtypeStruct((m, n // 2), F32)),
        grid=(m // tm,),
        in_specs=[pl.BlockSpec((tm, k), lambda i: (i, 0)),
                  pl.BlockSpec((k, n), lambda i: (0, 0)),
                  pl.BlockSpec((1, n // 2), lambda i: (0, 0)),
                  pl.BlockSpec((1, n // 2), lambda i: (0, 0)),
                  pl.BlockSpec((1, A_GROUPS, CHUNK, CHUNK), sel),
                  pl.BlockSpec((1, A_GROUPS, CHUNK, CHUNK), sel)],
        out_specs=(pl.BlockSpec((tm, n // 2), lambda i: (i, 0)),
                   pl.BlockSpec((tm, n // 2), lambda i: (i, 0))),
        compiler_params=_params("parallel"),
        name="proj_a",
    )(x, w, lng, lnb, ws2, bs2)


def _diff_lambda(lam_ref, lam_init):
    lv = lam_ref[...]
    s1 = jnp.sum(lv[0:1] * lv[1:2], axis=-1, keepdims=True)
    s2 = jnp.sum(lv[2:3] * lv[3:4], axis=-1, keepdims=True)
    return jnp.exp(s1) - jnp.exp(s2) + lam_init


def _stack_maps(q):
    lane = lax.broadcasted_iota(jnp.int32, q.shape, 1)
    zero = jnp.zeros((), q.dtype)
    return jnp.concatenate([jnp.where(lane < B_HEAD_DIM, q, zero),
                            jnp.where(lane >= B_HEAD_DIM, q, zero)], axis=0)


def _sub_norm(o, g, lam_init):
    r = lax.rsqrt(jnp.mean(o * o, axis=-1, keepdims=True) + LN_EPS)
    return o * r * g * (1.0 - lam_init)


def _attn_p_kernel(lam_ref, q_ref, k_ref, v_ref, g_ref, o_ref, kb_scr, vb_scr, *, tq, lam_init):
    qi = pl.program_id(2)

    @pl.when(qi == 0)
    def _():
        kb_scr[...] = k_ref[...].astype(BF16)
        vb_scr[...] = v_ref[...].astype(BF16)

    qs = _stack_maps(q_ref[...])

    def update(carry, s, vs):
        m, l, acc = carry
        m_new = jnp.maximum(m, jnp.max(s, axis=-1, keepdims=True))
        a = jnp.exp(m - m_new)
        p = jnp.exp(s - m_new)
        l = a * l + jnp.sum(p, axis=-1, keepdims=True)
        acc = a * acc + _dot(p.astype(BF16), vs)
        return m_new, l, acc

    def body(kb, carry):
        off = pl.multiple_of(kb * tq, tq)
        ks = kb_scr[pl.ds(off, tq), :]
        vs = vb_scr[pl.ds(off, tq), :]
        return update(carry, _dot_t(qs, ks), vs)

    init = (jnp.full((2 * tq, 1), -jnp.inf, F32), jnp.zeros((2 * tq, 1), F32),
            jnp.zeros((2 * tq, B_V_DIM), F32))
    carry = lax.fori_loop(0, qi, body, init)
    off = pl.multiple_of(qi * tq, tq)
    s = _dot_t(qs, kb_scr[pl.ds(off, tq), :])
    row = lax.broadcasted_iota(jnp.int32, s.shape, 0) % tq
    col = lax.broadcasted_iota(jnp.int32, s.shape, 1)
    s = jnp.where(col <= row, s, -jnp.inf)
    m, l, acc = update(carry, s, vb_scr[pl.ds(off, tq), :])
    o = acc / l
    lam = _diff_lambda(lam_ref, lam_init)
    o = o[:tq] - lam * o[tq:]
    o_ref[...] = _sub_norm(o, g_ref[...], lam_init).astype(o_ref.dtype)


def _attn_prompt(lamv, q, k, v, g, batch, seq, n_heads, lam_init, *, tq=256):
    tq = _tile(seq, tq)
    nq = seq // tq
    hd = B_V_DIM
    return pl.pallas_call(
        functools.partial(_attn_p_kernel, tq=tq, lam_init=lam_init),
        out_shape=jax.ShapeDtypeStruct((batch * seq, n_heads * hd), BF16),
        grid=(batch, n_heads, nq),
        in_specs=[pl.BlockSpec(lamv.shape, lambda b, h, i: (0, 0)),
                  pl.BlockSpec((tq, hd), lambda b, h, i: (b * nq + i, h)),
                  pl.BlockSpec((seq, hd), lambda b, h, i: (b, h)),
                  pl.BlockSpec((seq, hd), lambda b, h, i: (b, h)),
                  pl.BlockSpec((1, hd), lambda b, h, i: (0, 0))],
        out_specs=pl.BlockSpec((tq, hd), lambda b, h, i: (b * nq + i, h)),
        scratch_shapes=[pltpu.VMEM((seq, hd), BF16), pltpu.VMEM((seq, hd), BF16)],
        compiler_params=_params("parallel", "parallel", "arbitrary"),
        name="attn_prompt",
    )(lamv, q, k, v, g)


def _attn_s_kernel(pt_ref, lam_ref, q_ref, kn_ref, vn_ref, g_ref, *refs, n_pages, n_heads, page, dec,
                   lam_init):
    kp = refs[:n_pages]
    vp = refs[n_pages:2 * n_pages]
    o_ref = refs[2 * n_pages]
    s_scr, kn_scr, vn_scr = refs[2 * n_pages + 1:]

    @pl.when(pl.program_id(0) == 0)
    def _():
        kn_scr[...] = jnp.zeros_like(kn_scr)
        vn_scr[...] = jnp.zeros_like(vn_scr)

    kn_scr[0:dec, :] = kn_ref[0]
    vn_scr[0:dec, :] = vn_ref[0]
    lam = _diff_lambda(lam_ref, lam_init)
    past = n_pages * page
    for h in range(n_heads):
        cols = slice(h * B_V_DIM, (h + 1) * B_V_DIM)
        qs = _stack_maps(q_ref[0, :, cols])
        for p in range(n_pages):
            s_scr[:, p * page:(p + 1) * page] = _dot_t(qs, kp[p][0, :, cols].astype(BF16))
        sn = _dot_t(qs, kn_scr[:, cols].astype(BF16))
        row = lax.broadcasted_iota(jnp.int32, sn.shape, 0) % dec
        col = lax.broadcasted_iota(jnp.int32, sn.shape, 1)
        s_scr[:, past:] = jnp.where(col <= row, sn, -jnp.inf)
        s = s_scr[...]
        m = jnp.max(s, axis=-1, keepdims=True)
        e = jnp.exp(s - m)
        pr = (e / jnp.sum(e, axis=-1, keepdims=True)).astype(BF16)
        o = _dot(pr[:, past:], vn_scr[:, cols].astype(BF16))
        for p in range(n_pages):
            o = o + _dot(pr[:, p * page:(p + 1) * page], vp[p][0, :, cols].astype(BF16))
        o = o[:dec] - lam * o[dec:]
        o_ref[0, :, cols] = _sub_norm(o, g_ref[...], lam_init).astype(o_ref.dtype)


def _attn_sample(page_table, lamv, q, k_new, v_new, g, cache_k, cache_v, lam_init):
    n_seq, dec, width = q.shape
    n_pages = page_table.shape[1]
    page = cache_k.shape[1]
    n_heads = width // B_V_DIM
    tok = lambda b, pt: (b, 0, 0)
    const = lambda b, pt: (0, 0)

    def page_spec(p):
        return pl.BlockSpec((1, page, width), lambda b, pt: (pt[b, p], 0, 0))

    grid_spec = pltpu.PrefetchScalarGridSpec(
        num_scalar_prefetch=1,
        grid=(n_seq,),
        in_specs=[pl.BlockSpec(lamv.shape, const),
                  pl.BlockSpec((1, dec, width), tok),
                  pl.BlockSpec((1, dec, width), tok),
                  pl.BlockSpec((1, dec, width), tok),
                  pl.BlockSpec((1, B_V_DIM), const)]
                 + [page_spec(p) for p in range(n_pages)] * 1
                 + [page_spec(p) for p in range(n_pages)],
        out_specs=pl.BlockSpec((1, dec, width), tok),
        scratch_shapes=[pltpu.VMEM((2 * dec, (n_pages + 1) * page), F32),
                        pltpu.VMEM((page, width), F32),
                        pltpu.VMEM((page, width), F32)],
    )
    return pl.pallas_call(
        functools.partial(_attn_s_kernel, n_pages=n_pages, n_heads=n_heads, page=page, dec=dec,
                          lam_init=lam_init),
        out_shape=jax.ShapeDtypeStruct((n_seq, dec, width), BF16),
        grid_spec=grid_spec,
        compiler_params=_params("arbitrary"),
        name="attn_sample",
    )(page_table, lamv, q, k_new, v_new, g, *([cache_k] * n_pages), *([cache_v] * n_pages))


def _mem_attn_kernel(q_ref, k_ref, v_ref, o_ref, *, n_heads):
    tq = q_ref.shape[1]
    pad = (-tq) % 16
    for h in range(n_heads):
        cols = slice(h * C_HEAD_DIM, (h + 1) * C_HEAD_DIM)
        q = q_ref[0, :, cols]
        if pad:
            q = jnp.concatenate([q, jnp.zeros((pad, C_HEAD_DIM), q.dtype)], axis=0)
        s = _dot_t(q, k_ref[0, :, cols].astype(BF16)) * (C_HEAD_DIM ** -0.5)
        m = jnp.max(s, axis=-1, keepdims=True)
        e = jnp.exp(s - m)
        pr = (e / jnp.sum(e, axis=-1, keepdims=True)).astype(BF16)
        o = _dot(pr, v_ref[0, :, cols].astype(BF16))
        o_ref[0, :, cols] = o[:tq].astype(o_ref.dtype)


def _mem_attn(q, mk, mv, *, tq=512):
    b, t, w = q.shape
    n_mem = mk.shape[1]
    tq = _tile(t, tq)
    return pl.pallas_call(
        functools.partial(_mem_attn_kernel, n_heads=w // C_HEAD_DIM),
        out_shape=jax.ShapeDtypeStruct((b, t, w), BF16),
        grid=(b, t // tq),
        in_specs=[pl.BlockSpec((1, tq, w), lambda i, j: (i, j, 0)),
                  pl.BlockSpec((1, n_mem, w), lambda i, j: (i, 0, 0)),
                  pl.BlockSpec((1, n_mem, w), lambda i, j: (i, 0, 0))],
        out_specs=pl.BlockSpec((1, tq, w), lambda i, j: (i, j, 0)),
        compiler_params=_params("parallel", "arbitrary"),
        name="mem_attn",
    )(q, mk, mv)


def _merge_kernel(oa_ref, ob_ref, oc_ref, ga_ref, gb_ref, gc_ref, wa_ref, wb_ref, wc_ref, m_ref):
    m = ga_ref[...].astype(F32) * _dot(oa_ref[...], wa_ref[...])
    m = m + gb_ref[...].astype(F32) * _dot(ob_ref[...], wb_ref[...])
    m = m + gc_ref[...].astype(F32) * _dot(oc_ref[...], wc_ref[...])
    m_ref[...] = m.astype(m_ref.dtype)


def _merge(oa, ob, oc, gates, wa, wb, wc, *, tm=1024, tn=512):
    t = oa.shape[0]
    d = wa.shape[1]
    tm, tn = _tile(t, tm), _tile(d, tn)
    nj = d // tn
    row = lambda width: pl.BlockSpec((tm, width), lambda i, j: (i, 0))
    gate = lambda b: pl.BlockSpec((tm, tn), lambda i, j: (i, b * nj + j))
    wcol = lambda width: pl.BlockSpec((width, tn), lambda i, j: (0, j))
    return pl.pallas_call(
        _merge_kernel,
        out_shape=jax.ShapeDtypeStruct((t, d), BF16),
        grid=(t // tm, nj),
        in_specs=[row(oa.shape[1]), row(ob.shape[1]), row(oc.shape[1]), gate(0), gate(1), gate(2),
                  wcol(wa.shape[0]), wcol(wb.shape[0]), wcol(wc.shape[0])],
        out_specs=pl.BlockSpec((tm, tn), lambda i, j: (i, j)),
        compiler_params=_params("parallel", "arbitrary"),
        name="merge",
    )(oa, ob, oc, gates, gates, gates, wa, wb, wc)


def _layer_norm(x, g, b):
    mu = jnp.mean(x, axis=-1, keepdims=True)
    var = jnp.mean(jnp.square(x - mu), axis=-1, keepdims=True)
    return (x - mu) * lax.rsqrt(var + LN_EPS) * g + b


def _out_kernel(m_ref, w_ref, x_ref, g_ref, b_ref, h_ref, hb_ref, *, alpha):
    y = alpha * x_ref[...] + _dot(m_ref[...], w_ref[...])
    h = _layer_norm(y, g_ref[...], b_ref[...])
    h_ref[...] = h
    hb_ref[...] = h.astype(BF16)


def _out_proj(m, w_o, x, g, b, alpha, *, tm=512):
    t, d = x.shape
    tm = _tile(t, tm)
    row = pl.BlockSpec((tm, d), lambda i: (i, 0))
    vec = pl.BlockSpec((1, d), lambda i: (0, 0))
    return pl.pallas_call(
        functools.partial(_out_kernel, alpha=alpha),
        out_shape=(jax.ShapeDtypeStruct((t, d), F32), jax.ShapeDtypeStruct((t, d), BF16)),
        grid=(t // tm,),
        in_specs=[row, pl.BlockSpec((d, d), lambda i: (0, 0)), row, vec, vec],
        out_specs=(row, row),
        compiler_params=_params("parallel"),
        name="out_proj_ln1",
    )(m, w_o, x, g, b)


def _first_argmax(vals, idx, sentinel):
    m = jnp.max(vals, axis=0, keepdims=True)
    first = jnp.min(jnp.where(vals == m, idx, sentinel), axis=0, keepdims=True)
    return m, first


def _router_kernel(x_ref, w_ref, bias_ref, idx_ref, wt_ref):
    s = jax.nn.sigmoid(_dot(x_ref[...], w_ref[...]))
    st = s.T
    sb = st + bias_ref[...]
    n_exp, tm = st.shape
    gs = n_exp // N_GROUPS
    neg = -jnp.inf
    ridx = lax.broadcasted_iota(jnp.int32, (gs, tm), 0)
    grp = []
    for g in range(N_GROUPS):
        blk = sb[g * gs:(g + 1) * gs]
        m1, first = _first_argmax(blk, ridx, gs)
        m2 = jnp.max(jnp.where(ridx == first, neg, blk), axis=0, keepdims=True)
        grp.append(m1 + m2)
    work = jnp.concatenate(grp, axis=0)
    gidx = lax.broadcasted_iota(jnp.int32, work.shape, 0)
    gmask = jnp.zeros(work.shape, jnp.bool_)
    for _ in range(TOPK_GROUPS):
        _, first = _first_argmax(work, gidx, N_GROUPS)
        sel = gidx == first
        gmask = jnp.logical_or(gmask, sel)
        work = jnp.where(sel, neg, work)
    masked = jnp.concatenate(
        [jnp.where(gmask[g:g + 1], sb[g * gs:(g + 1) * gs], neg) for g in range(N_GROUPS)], axis=0)
    eidx = lax.broadcasted_iota(jnp.int32, masked.shape, 0)
    ids, wts = [], []
    for _ in range(TOP_K):
        _, first = _first_argmax(masked, eidx, n_exp)
        sel = eidx == first
        ids.append(first)
        wts.append(jnp.sum(jnp.where(sel, st, 0.0), axis=0, keepdims=True))
        masked = jnp.where(sel, neg, masked)
    wt = jnp.concatenate(wts, axis=0)
    idx_ref[...] = jnp.concatenate(ids, axis=0)
    wt_ref[...] = wt / jnp.sum(wt, axis=0, keepdims=True) * ROUTED_SCALE


def _router(hb, w_router, bias, *, tm=256):
    t, d = hb.shape
    n_exp = w_router.shape[1]
    tm = _tile(t, tm)
    return pl.pallas_call(
        _router_kernel,
        out_shape=(jax.ShapeDtypeStruct((TOP_K, t), jnp.int32), jax.ShapeDtypeStruct((TOP_K, t), F32)),
        grid=(t // tm,),
        in_specs=[pl.BlockSpec((tm, d), lambda i: (i, 0)),
                  pl.BlockSpec((d, n_exp), lambda i: (0, 0)),
                  pl.BlockSpec((n_exp, 1), lambda i: (0, 0))],
        out_specs=(pl.BlockSpec((TOP_K, tm), lambda i: (0, i)),
                   pl.BlockSpec((TOP_K, tm), lambda i: (0, i))),
        compiler_params=_params("parallel"),
        name="router",
    )(hb, w_router, bias)


def _expert_kernel(be_ref, nu_ref, x_ref, rw_ref, wg_ref, wu_ref, wd_ref, y_ref, wg_scr, wu_scr, wd_scr):
    i = pl.program_id(0)
    e = be_ref[i]
    prev = be_ref[jnp.maximum(i - 1, 0)]

    @pl.when(jnp.logical_or(i == 0, e != prev))
    def _():
        wg_scr[...] = wg_ref[0].astype(BF16)
        wu_scr[...] = wu_ref[0].astype(BF16)
        wd_scr[...] = wd_ref[0].astype(BF16)

    @pl.when(i < nu_ref[0])
    def _():
        x = x_ref[...]
        a = jax.nn.silu(_dot(x, wg_scr[...])) * _dot(x, wu_scr[...])
        y = _dot(a.astype(BF16), wd_scr[...])
        y_ref[...] = (y * rw_ref[...]).astype(y_ref.dtype)

    @pl.when(i >= nu_ref[0])
    def _():
        y_ref[...] = jnp.zeros_like(y_ref)


def _experts(blk_e, n_used, x_sorted, row_w, w_gate, w_up, w_down):
    n_rows, d = x_sorted.shape
    f = w_gate.shape[2]
    n_blocks = n_rows // MOE_ROWS
    wspec = lambda shape: pl.BlockSpec((1,) + shape, lambda i, be, nu: (be[i], 0, 0))
    grid_spec = pltpu.PrefetchScalarGridSpec(
        num_scalar_prefetch=2,
        grid=(n_blocks,),
        in_specs=[pl.BlockSpec((MOE_ROWS, d), lambda i, be, nu: (i, 0)),
                  pl.BlockSpec((MOE_ROWS, 1), lambda i, be, nu: (i, 0)),
                  wspec((d, f)), wspec((d, f)), wspec((f, d))],
        out_specs=pl.BlockSpec((MOE_ROWS, d), lambda i, be, nu: (i, 0)),
        scratch_shapes=[pltpu.VMEM((d, f), BF16), pltpu.VMEM((d, f), BF16), pltpu.VMEM((f, d), BF16)],
    )
    return pl.pallas_call(
        _expert_kernel,
        out_shape=jax.ShapeDtypeStruct((n_rows, d), BF16),
        grid_spec=grid_spec,
        compiler_params=_params("arbitrary"),
        name="experts",
    )(blk_e, n_used, x_sorted, row_w, w_gate, w_up, w_down)


def _final_kernel(h_ref, hb_ref, yg_ref, wg_ref, wu_ref, wd_ref, g_ref, b_ref, o_ref, *, alpha):
    hb = hb_ref[...]
    a = jax.nn.silu(_dot(hb, wg_ref[...])) * _dot(hb, wu_ref[...])
    shared = _dot(a.astype(BF16), wd_ref[...])
    routed = yg_ref[0].astype(F32)
    for k in range(1, yg_ref.shape[0]):
        routed = routed + yg_ref[k].astype(F32)
    y = alpha * h_ref[...] + (routed + shared)
    o_ref[...] = _layer_norm(y, g_ref[...], b_ref[...])


def _final(h, hb, yg, ws_gate, ws_up, ws_down, g, b, alpha, *, tm=256):
    t, d = h.shape
    f = ws_gate.shape[1]
    tm = _tile(t, tm)
    row = pl.BlockSpec((tm, d), lambda i: (i, 0))
    vec = pl.BlockSpec((1, d), lambda i: (0, 0))
    return pl.pallas_call(
        functools.partial(_final_kernel, alpha=alpha),
        out_shape=jax.ShapeDtypeStruct((t, d), F32),
        grid=(t // tm,),
        in_specs=[row, row, pl.BlockSpec((yg.shape[0], tm, d), lambda i: (0, i, 0)),
                  pl.BlockSpec((d, f), lambda i: (0, 0)), pl.BlockSpec((d, f), lambda i: (0, 0)),
                  pl.BlockSpec((f, d), lambda i: (0, 0)), vec, vec],
        out_specs=row,
        compiler_params=_params("parallel"),
        name="shared_combine_ln2",
    )(h, hb, yg, ws_gate, ws_up, ws_down, g, b)


def _moe(h, hb, w_router, router_bias, w_gate, w_up, w_down, ws_gate, ws_up, ws_down, g, b, alpha):
    t, d = h.shape
    n_exp = w_router.shape[1]
    idx_t, wt_t = _router(hb, w_router, router_bias.reshape(n_exp, 1))
    idx = idx_t.T
    wt = wt_t.T
    tk = t * TOP_K
    flat_e = idx.reshape(-1)
    order = jnp.argsort(flat_e)
    se = flat_e[order]
    counts = jnp.bincount(flat_e, length=n_exp)
    pcounts = (counts + MOE_ROWS - 1) // MOE_ROWS * MOE_ROWS
    pend = jnp.cumsum(pcounts)
    dest = (pend - pcounts)[se] + jnp.arange(tk) - (jnp.cumsum(counts) - counts)[se]
    n_blocks = -(-tk // MOE_ROWS) + n_exp
    n_rows = n_blocks * MOE_ROWS
    row_tok = jnp.full((n_rows,), t, jnp.int32).at[dest].set((order // TOP_K).astype(jnp.int32))
    row_w = jnp.zeros((n_rows,), F32).at[dest].set(wt.reshape(-1)[order])
    slot_row = jnp.zeros((tk,), jnp.int32).at[order].set(dest.astype(jnp.int32))
    blk_e = jnp.minimum(jnp.searchsorted(pend, jnp.arange(n_blocks) * MOE_ROWS, side="right"),
                        n_exp - 1).astype(jnp.int32)
    n_used = (pend[-1:] // MOE_ROWS).astype(jnp.int32)
    x_pad = jnp.concatenate([hb, jnp.zeros((1, d), hb.dtype)], axis=0)
    x_sorted = x_pad[row_tok]
    y_sorted = _experts(blk_e, n_used, x_sorted, row_w.reshape(n_rows, 1), w_gate, w_up, w_down)
    yg = y_sorted[slot_row.reshape(t, TOP_K).T]
    return _final(h, hb, yg, ws_gate, ws_up, ws_down, g, b, alpha)


def _rope_tables(pos):
    half = B_HEAD_DIM // 2
    inv = 1.0 / (ROPE_THETA ** (jnp.arange(half, dtype=F32) / half))
    ang = pos.astype(F32)[:, None] * inv[None, :]
    cos, sin = jnp.cos(ang), jnp.sin(ang)
    reps = LANES // B_HEAD_DIM
    return (jnp.concatenate([cos, cos] * reps, axis=1), jnp.concatenate([-sin, sin] * reps, axis=1))


def kernel(x_prompt, x_sample, mem_prompt, cache_k, cache_v, page_table, cache_mem_k, cache_mem_v, w_in, a_ln_g, a_ln_b, a_ws, a_bs, lam_q1, lam_k1, lam_q2, lam_k2, b_subln_g, w_mk, w_mv, w_pa, w_pb, w_pc, w_o, ln1_g, ln1_b, w_router, router_bias, w_gate, w_up, w_down, ws_gate, ws_up, ws_down, ln2_g, ln2_b):
    depth = w_in.shape[0]
    assert depth == 1, "single-layer step"
    layer = 0
    batch, seq, d = x_prompt.shape
    n_seq, dec, _ = x_sample.shape
    n_mem = mem_prompt.shape[1]
    n_phys, page, n_bh = cache_k.shape[1:4]
    past = page_table.shape[1] * page
    c_width = cache_mem_k.shape[3] * C_HEAD_DIM
    a_width = A_GROUPS * A_GROUP_DIM
    qk_width = n_bh * 2 * B_HEAD_DIM
    bv_width = n_bh * B_V_DIM
    assert CHUNK % dec == 0 and seq % CHUNK == 0
    alpha = (2.0 * depth) ** 0.25
    lam_init = _lambda_init(layer)
    tp, ts = batch * seq, n_seq * dec
    t = tp + ts

    x = jnp.concatenate([x_prompt.reshape(tp, d), x_sample.reshape(ts, d)], axis=0)
    xb = x.astype(BF16)
    w_in_b = w_in[layer].astype(BF16)
    o = [0, 2 * a_width]
    for wdt in (qk_width, qk_width, bv_width, c_width, 3 * d):
        o.append(o[-1] + wdt)
    pos = jnp.concatenate([jnp.tile(jnp.arange(seq), batch), past + jnp.tile(jnp.arange(dec), n_seq)])
    rope = _rope_tables(pos)

    reps = CHUNK // dec
    ws = a_ws[layer]
    ws_s = jnp.einsum("ab,gts->gatbs", jnp.eye(reps, dtype=ws.dtype), ws[:, :dec, :dec]).reshape(
        A_GROUPS, CHUNK, CHUNK)
    ws2 = jnp.stack([ws, ws_s]).astype(BF16)
    bs = a_bs[layer]
    bs2 = jnp.stack([bs, jnp.tile(bs[:, :dec], (1, reps))])
    bs2 = jnp.broadcast_to(bs2[..., None], bs2.shape + (CHUNK,)).astype(F32)

    o_a, vn = _proj_a(xb, w_in_b[:, o[0]:o[1]], a_ln_g[layer].reshape(1, a_width),
                      a_ln_b[layer].reshape(1, a_width), ws2, bs2, tp)
    q = _mm(xb, w_in_b[:, o[1]:o[2]], BF16, epilogue="rope", scale=B_HEAD_DIM ** -0.5, rope=rope)
    k = _mm(xb, w_in_b[:, o[2]:o[3]], F32, epilogue="rope", rope=rope)
    v = _mm(xb, w_in_b[:, o[3]:o[4]], F32)
    cq = _mm(xb, w_in_b[:, o[4]:o[5]], BF16)
    gates = _mm(xb, w_in_b[:, o[5]:o[6]], BF16, epilogue="sigmoid")

    lamv = jnp.stack([lam_q1[layer], lam_k1[layer], lam_q2[layer], lam_k2[layer]]).astype(F32)
    g_sub = b_subln_g[layer].reshape(1, B_V_DIM)
    ob_p = _attn_prompt(lamv, q[:tp], k[:tp], v[:tp], g_sub, batch, seq, n_bh, lam_init)
    ob_s = _attn_sample(page_table, lamv, q[tp:].reshape(n_seq, dec, qk_width),
                        k[tp:].reshape(n_seq, dec, qk_width), v[tp:].reshape(n_seq, dec, bv_width), g_sub,
                        cache_k[layer].reshape(n_phys, page, qk_width),
                        cache_v[layer].reshape(n_phys, page, bv_width), lam_init)
    o_b = jnp.concatenate([ob_p, ob_s.reshape(ts, bv_width)], axis=0)

    memb = mem_prompt.reshape(batch * n_mem, d).astype(BF16)
    mk = _mm(memb, w_mk[layer].astype(BF16), F32)
    mv = _mm(memb, w_mv[layer].astype(BF16), F32)
    oc_p = _mem_attn(cq[:tp].reshape(batch, seq, c_width), mk.reshape(batch, n_mem, c_width),
                     mv.reshape(batch, n_mem, c_width))
    oc_s = _mem_attn(cq[tp:].reshape(n_seq, dec, c_width), cache_mem_k[layer].reshape(n_seq, n_mem, c_width),
                     cache_mem_v[layer].reshape(n_seq, n_mem, c_width))
    o_c = jnp.concatenate([oc_p.reshape(tp, c_width), oc_s.reshape(ts, c_width)], axis=0)

    m = _merge(o_a, o_b, o_c, gates, w_pa[layer].astype(BF16), w_pb[layer].astype(BF16),
               w_pc[layer].astype(BF16))
    h1, h1b = _out_proj(m, w_o[layer].astype(BF16), x, ln1_g[layer].reshape(1, d), ln1_b[layer].reshape(1, d),
                        alpha)
    y = _moe(h1, h1b, w_router[layer].astype(BF16), router_bias[layer], w_gate[layer], w_up[layer],
             w_down[layer], ws_gate[layer].astype(BF16), ws_up[layer].astype(BF16),
             ws_down[layer].astype(BF16), ln2_g[layer].reshape(1, d), ln2_b[layer].reshape(1, d), alpha)

    c_heads = c_width // C_HEAD_DIM
    return (y[:tp].reshape(batch, seq, d),
            y[tp:].reshape(n_seq, dec, d),
            k[:tp].reshape(1, batch, seq, n_bh, 2, B_HEAD_DIM),
            v[:tp].reshape(1, batch, seq, n_bh, B_V_DIM),
            mk.reshape(1, batch, n_mem, c_heads, C_HEAD_DIM),
            mv.reshape(1, batch, n_mem, c_heads, C_HEAD_DIM),
            k[tp:].reshape(1, n_seq, dec, n_bh, 2, B_HEAD_DIM),
            v[tp:].reshape(1, n_seq, dec, n_bh, B_V_DIM),
            vn[tp:].reshape(1, n_seq, dec, a_width))
```

```python
import functools
import math

import jax
import jax.numpy as jnp
from jax import lax
from jax.experimental import pallas as pl
from jax.experimental.pallas import tpu as pltpu

F32 = jnp.float32
BF16 = jnp.bfloat16

LN_EPS = 1e-5
ROPE_THETA = 10000.0
LANES = 128
A_GROUP_DIM = 128
A_GROUPS = 4
CHUNK = 128
B_HEAD_DIM = 64
B_V_DIM = 128
C_HEAD_DIM = 128
N_GROUPS = 8
TOPK_GROUPS = 4
TOP_K = 8
ROUTED_SCALE = 2.5
MOE_ROWS = 256
VMEM_LIMIT = 56 * 1024 * 1024


def _lambda_init(layer):
    return 0.8 - 0.6 * math.exp(-0.3 * layer)


def _params(*sem):
    return pltpu.CompilerParams(dimension_semantics=sem, vmem_limit_bytes=VMEM_LIMIT)


def _tile(n, pref):
    t = min(n, pref)
    while n % t:
        t //= 2
    return t


def _dot(a, b):
    return jnp.dot(a, b, preferred_element_type=F32)


def _dot_t(a, b):
    return lax.dot_general(a, b, (((1,), (1,)), ((), ())), preferred_element_type=F32)


def _mm_kernel(*refs, epilogue, scale):
    if epilogue == "rope":
        x_ref, w_ref, cos_ref, sin_ref, o_ref = refs
    else:
        x_ref, w_ref, o_ref = refs
    acc = _dot(x_ref[...], w_ref[...])
    if epilogue == "rope":
        n = acc.shape[1] // LANES
        cos = jnp.concatenate([cos_ref[...]] * n, axis=1)
        sin = jnp.concatenate([sin_ref[...]] * n, axis=1)
        lane = lax.broadcasted_iota(jnp.int32, acc.shape, 1)
        first = (lane % B_HEAD_DIM) < (B_HEAD_DIM // 2)
        half = B_HEAD_DIM // 2
        swapped = jnp.where(first, pltpu.roll(acc, acc.shape[1] - half, 1), pltpu.roll(acc, half, 1))
        acc = acc * cos + swapped * sin
    elif epilogue == "sigmoid":
        acc = jax.nn.sigmoid(acc)
    if scale != 1.0:
        acc = acc * scale
    o_ref[...] = acc.astype(o_ref.dtype)


def _mm(x, w, out_dtype, *, epilogue="none", scale=1.0, rope=None, tm=512, tn=512):
    m, k = x.shape
    n = w.shape[1]
    tm, tn = _tile(m, tm), _tile(n, tn)
    in_specs = [pl.BlockSpec((tm, k), lambda i, j: (i, 0)),
                pl.BlockSpec((k, tn), lambda i, j: (0, j))]
    args = [x, w]
    if epilogue == "rope":
        in_specs += [pl.BlockSpec((tm, LANES), lambda i, j: (i, 0))] * 2
        args += list(rope)
    return pl.pallas_call(
        functools.partial(_mm_kernel, epilogue=epilogue, scale=scale),
        out_shape=jax.ShapeDtypeStruct((m, n), out_dtype),
        grid=(m // tm, n // tn),
        in_specs=in_specs,
        out_specs=pl.BlockSpec((tm, tn), lambda i, j: (i, j)),
        compiler_params=_params("parallel", "arbitrary"),
        name="proj_" + epilogue,
    )(*args)


def _kt_kernel(w_ref, x_ref, cos_ref, sin_ref, o_ref):
    acc = _dot_t(w_ref[...], x_ref[...])
    cos, sin = cos_ref[...], sin_ref[...]
    half = B_HEAD_DIM // 2
    for g in range(acc.shape[0] // B_HEAD_DIM):
        lo = g * B_HEAD_DIM
        x1 = acc[lo:lo + half]
        x2 = acc[lo + half:lo + B_HEAD_DIM]
        o_ref[0, lo:lo + half, :] = x1 * cos - x2 * sin
        o_ref[0, lo + half:lo + B_HEAD_DIM, :] = x2 * cos + x1 * sin


def _proj_kt(w_t, x, cos_t, sin_t, n_groups, m, *, tm):
    n, k = w_t.shape
    per = m // n_groups
    tm = _tile(per, tm)
    nt = per // tm
    half = B_HEAD_DIM // 2
    return pl.pallas_call(
        _kt_kernel,
        out_shape=jax.ShapeDtypeStruct((n_groups, n, per), F32),
        grid=(m // tm,),
        in_specs=[pl.BlockSpec((n, k), lambda i: (0, 0)),
                  pl.BlockSpec((tm, k), lambda i: (i, 0)),
                  pl.BlockSpec((half, tm), lambda i: (0, i)),
                  pl.BlockSpec((half, tm), lambda i: (0, i))],
        out_specs=pl.BlockSpec((1, n, tm), lambda i: (i // nt, 0, i % nt)),
        compiler_params=_params("parallel"),
        name="proj_kt",
    )(w_t, x, cos_t, sin_t)


def _proj_a_kernel(x_ref, w_ref, lng_ref, lnb_ref, ws_ref, bs_ref, oa_ref, vn_ref):
    h = _dot(x_ref[...], w_ref[...])
    aw = h.shape[1] // 2
    u = jax.nn.gelu(h[:, :aw])
    g = jax.nn.gelu(h[:, aw:])
    mu = jnp.mean(g, axis=-1, keepdims=True)
    var = jnp.mean(jnp.square(g - mu), axis=-1, keepdims=True)
    vn = (g - mu) * lax.rsqrt(var + LN_EPS) * lng_ref[...] + lnb_ref[...]
    vn_ref[...] = vn
    vnb = vn.astype(BF16)
    row = lax.broadcasted_iota(jnp.int32, (CHUNK, CHUNK), 0)
    col = lax.broadcasted_iota(jnp.int32, (CHUNK, CHUNK), 1)
    causal = col <= row
    for gi in range(A_GROUPS):
        wsg = jnp.where(causal, ws_ref[0, gi], jnp.zeros((), BF16))
        cols = slice(gi * A_GROUP_DIM, (gi + 1) * A_GROUP_DIM)
        for c in range(h.shape[0] // CHUNK):
            rows = slice(c * CHUNK, (c + 1) * CHUNK)
            s = _dot(wsg, vnb[rows, cols]) + bs_ref[0, gi]
            oa_ref[rows, cols] = (u[rows, cols] * s).astype(oa_ref.dtype)


def _proj_a(x, w, lng, lnb, ws2, bs2, n_prompt, *, tm=512):
    m, k = x.shape
    n = w.shape[1]
    tm = _tile(math.gcd(m, n_prompt), tm)
    npt = n_prompt // tm
    sel = lambda i: (jnp.where(i >= npt, 1, 0), 0, 0, 0)
    return pl.pallas_call(
        _proj_a_kernel,
        out_shape=(jax.ShapeDtypeStruct((m, n // 2), BF16), jax.ShapeDtypeStruct((m, n // 2), F32)),
        grid=(m // tm,),
        in_specs=[pl.BlockSpec((tm, k), lambda i: (i, 0)),
                  pl.BlockSpec((k, n), lambda i: (0, 0)),
                  pl.BlockSpec((1, n // 2), lambda i: (0, 0)),
                  pl.BlockSpec((1, n // 2), lambda i: (0, 0)),
                  pl.BlockSpec((1, A_GROUPS, CHUNK, CHUNK), sel),
                  pl.BlockSpec((1, A_GROUPS, CHUNK, CHUNK), sel)],
        out_specs=(pl.BlockSpec((tm, n // 2), lambda i: (i, 0)),
                   pl.BlockSpec((tm, n // 2), lambda i: (i, 0))),
        compiler_params=_params("parallel"),
        name="proj_a",
    )(x, w, lng, lnb, ws2, bs2)


def _diff_lambda(lam_ref, lam_init):
    lv = lam_ref[...]
    s1 = jnp.sum(lv[0:1] * lv[1:2], axis=-1, keepdims=True)
    s2 = jnp.sum(lv[2:3] * lv[3:4], axis=-1, keepdims=True)
    return jnp.exp(s1) - jnp.exp(s2) + lam_init


def _stack_maps(q):
    lane = lax.broadcasted_iota(jnp.int32, q.shape, 1)
    zero = jnp.zeros((), q.dtype)
    return jnp.concatenate([jnp.where(lane < B_HEAD_DIM, q, zero),
                            jnp.where(lane >= B_HEAD_DIM, q, zero)], axis=0)


def _sub_norm(o, g, lam_init):
    r = lax.rsqrt(jnp.mean(o * o, axis=-1, keepdims=True) + LN_EPS)
    return o * r * g * (1.0 - lam_init)


def _attn_p_kernel(lam_ref, q_ref, kt_ref, v_ref, g_ref, o_ref, kt_scr, vb_scr, *, tq, nh, lam_init):
    qi = pl.program_id(2)
    hd = B_V_DIM

    @pl.when(qi == 0)
    def _():
        for kb in range(kt_scr.shape[0]):
            kt_scr[kb] = kt_ref[0, :, kb * tq:(kb + 1) * tq].astype(BF16)
        vb_scr[...] = v_ref[...].astype(BF16)

    qs = [_stack_maps(q_ref[:, h * hd:(h + 1) * hd]) for h in range(nh)]

    def step(carry, kb, diag):
        off = pl.multiple_of(kb * tq, tq)
        kt = kt_scr[kb]
        out = []
        for h in range(nh):
            s = _dot(qs[h], kt[h * hd:(h + 1) * hd])
            if diag:
                row = lax.broadcasted_iota(jnp.int32, s.shape, 0) % tq
                col = lax.broadcasted_iota(jnp.int32, s.shape, 1)
                s = jnp.where(col <= row, s, -jnp.inf)
            m, l, acc = carry[h]
            m_new = jnp.maximum(m, jnp.max(s, axis=-1, keepdims=True))
            a = jnp.exp(m - m_new)
            p = jnp.exp(s - m_new)
            l = a * l + jnp.sum(p, axis=-1, keepdims=True)
            acc = a * acc + _dot(p.astype(BF16), vb_scr[pl.ds(off, tq), h * hd:(h + 1) * hd])
            out.append((m_new, l, acc))
        return tuple(out)

    init = tuple((jnp.full((2 * tq, 1), -jnp.inf, F32), jnp.zeros((2 * tq, 1), F32),
                  jnp.zeros((2 * tq, hd), F32)) for _ in range(nh))
    carry = lax.fori_loop(0, qi, lambda kb, c: step(c, kb, False), init)
    carry = step(carry, qi, True)
    lam = _diff_lambda(lam_ref, lam_init)
    for h in range(nh):
        _, l, acc = carry[h]
        o = acc / l
        o = o[:tq] - lam * o[tq:]
        o_ref[:, h * hd:(h + 1) * hd] = _sub_norm(o, g_ref[...], lam_init).astype(o_ref.dtype)


def _attn_prompt(lamv, q, kt, v, g, batch, seq, n_heads, lam_init, *, tq=256, nh=2):
    tq = _tile(seq, tq)
    nq = seq // tq
    nh = math.gcd(nh, n_heads)
    w = nh * B_V_DIM
    return pl.pallas_call(
        functools.partial(_attn_p_kernel, tq=tq, nh=nh, lam_init=lam_init),
        out_shape=jax.ShapeDtypeStruct((batch * seq, n_heads * B_V_DIM), BF16),
        grid=(batch, n_heads // nh, nq),
        in_specs=[pl.BlockSpec(lamv.shape, lambda b, h, i: (0, 0)),
                  pl.BlockSpec((tq, w), lambda b, h, i: (b * nq + i, h)),
                  pl.BlockSpec((1, w, seq), lambda b, h, i: (b, h, 0)),
                  pl.BlockSpec((seq, w), lambda b, h, i: (b, h)),
                  pl.BlockSpec((1, B_V_DIM), lambda b, h, i: (0, 0))],
        out_specs=pl.BlockSpec((tq, w), lambda b, h, i: (b * nq + i, h)),
        scratch_shapes=[pltpu.VMEM((nq, w, tq), BF16), pltpu.VMEM((seq, w), BF16)],
        compiler_params=_params("parallel", "parallel", "arbitrary"),
        name="attn_prompt",
    )(lamv, q, kt, v, g)


def _attn_s_kernel(pt_ref, lam_ref, q_ref, kn_ref, vn_ref, g_ref, *refs, n_pages, n_heads, page, dec,
                   lam_init):
    kp = refs[:n_pages]
    vp = refs[n_pages:2 * n_pages]
    o_ref = refs[2 * n_pages]
    s_scr, kn_scr, vn_scr = refs[2 * n_pages + 1:]

    @pl.when(pl.program_id(0) == 0)
    def _():
        kn_scr[...] = jnp.zeros_like(kn_scr)
        vn_scr[...] = jnp.zeros_like(vn_scr)

    kn_scr[0:dec, :] = kn_ref[0]
    vn_scr[0:dec, :] = vn_ref[0]
    lam = _diff_lambda(lam_ref, lam_init)
    past = n_pages * page
    for h in range(n_heads):
        cols = slice(h * B_V_DIM, (h + 1) * B_V_DIM)
        qs = _stack_maps(q_ref[0, :, cols])
        for p in range(n_pages):
            s_scr[:, p * page:(p + 1) * page] = _dot(qs, kp[p][0, h].astype(BF16))
        sn = _dot_t(qs, kn_scr[:, cols].astype(BF16))
        row = lax.broadcasted_iota(jnp.int32, sn.shape, 0) % dec
        col = lax.broadcasted_iota(jnp.int32, sn.shape, 1)
        s_scr[:, past:] = jnp.where(col <= row, sn, -jnp.inf)
        s = s_scr[...]
        m = jnp.max(s, axis=-1, keepdims=True)
        e = jnp.exp(s - m)
        pr = (e / jnp.sum(e, axis=-1, keepdims=True)).astype(BF16)
        o = _dot(pr[:, past:], vn_scr[:, cols].astype(BF16))
        for p in range(n_pages):
            vh = vp[p][0, pl.ds(h, page, stride=n_heads), :]
            o = o + _dot(pr[:, p * page:(p + 1) * page], vh.astype(BF16))
        o = o[:dec] - lam * o[dec:]
        o_ref[0, :, cols] = _sub_norm(o, g_ref[...], lam_init).astype(o_ref.dtype)


def _attn_sample(page_table, lamv, q, k_new, v_new, g, cache_kt, cache_v, lam_init):
    n_seq, dec, width = q.shape
    n_pages = page_table.shape[1]
    n_heads = width // B_V_DIM
    page = cache_kt.shape[3]
    tok = lambda b, pt: (b, 0, 0)
    const = lambda b, pt: (0, 0)

    def k_spec(p):
        return pl.BlockSpec((1, n_heads, B_V_DIM, page), lambda b, pt: (pt[b, p], 0, 0, 0))

    def v_spec(p):
        return pl.BlockSpec((1, page * n_heads, B_V_DIM), lambda b, pt: (pt[b, p], 0, 0))

    grid_spec = pltpu.PrefetchScalarGridSpec(
        num_scalar_prefetch=1,
        grid=(n_seq,),
        in_specs=[pl.BlockSpec(lamv.shape, const),
                  pl.BlockSpec((1, dec, width), tok),
                  pl.BlockSpec((1, dec, width), tok),
                  pl.BlockSpec((1, dec, width), tok),
                  pl.BlockSpec((1, B_V_DIM), const)]
                 + [k_spec(p) for p in range(n_pages)]
                 + [v_spec(p) for p in range(n_pages)],
        out_specs=pl.BlockSpec((1, dec, width), tok),
        scratch_shapes=[pltpu.VMEM((2 * dec, (n_pages + 1) * page), F32),
                        pltpu.VMEM((page, width), F32),
                        pltpu.VMEM((page, width), F32)],
    )
    return pl.pallas_call(
        functools.partial(_attn_s_kernel, n_pages=n_pages, n_heads=n_heads, page=page, dec=dec,
                          lam_init=lam_init),
        out_shape=jax.ShapeDtypeStruct((n_seq, dec, width), BF16),
        grid_spec=grid_spec,
        compiler_params=_params("arbitrary"),
        name="attn_sample",
    )(page_table, lamv, q, k_new, v_new, g, *([cache_kt] * n_pages), *([cache_v] * n_pages))


def _mem_attn_kernel(q_ref, k_ref, v_ref, o_ref, *, n_heads, interleaved):
    tq = q_ref.shape[1]
    pad = (-tq) % 16
    n_mem = k_ref.shape[1] // n_heads if interleaved else k_ref.shape[1]
    for h in range(n_heads):
        cols = slice(h * C_HEAD_DIM, (h + 1) * C_HEAD_DIM)
        if interleaved:
            kh = k_ref[0, pl.ds(h, n_mem, stride=n_heads), :]
            vh = v_ref[0, pl.ds(h, n_mem, stride=n_heads), :]
        else:
            kh, vh = k_ref[0, :, cols], v_ref[0, :, cols]
        q = q_ref[0, :, cols]
        if pad:
            q = jnp.concatenate([q, jnp.zeros((pad, C_HEAD_DIM), q.dtype)], axis=0)
        s = _dot_t(q, kh.astype(BF16)) * (C_HEAD_DIM ** -0.5)
        m = jnp.max(s, axis=-1, keepdims=True)
        e = jnp.exp(s - m)
        pr = (e / jnp.sum(e, axis=-1, keepdims=True)).astype(BF16)
        o = _dot(pr, vh.astype(BF16))
        o_ref[0, :, cols] = o[:tq].astype(o_ref.dtype)


def _mem_attn(q, mk, mv, *, interleaved, tq=512):
    b, t, w = q.shape
    tq = _tile(t, tq)
    kv_spec = pl.BlockSpec((1,) + mk.shape[1:], lambda i, j: (i, 0, 0))
    return pl.pallas_call(
        functools.partial(_mem_attn_kernel, n_heads=w // C_HEAD_DIM, interleaved=interleaved),
        out_shape=jax.ShapeDtypeStruct((b, t, w), BF16),
        grid=(b, t // tq),
        in_specs=[pl.BlockSpec((1, tq, w), lambda i, j: (i, j, 0)), kv_spec, kv_spec],
        out_specs=pl.BlockSpec((1, tq, w), lambda i, j: (i, j, 0)),
        compiler_params=_params("parallel", "arbitrary"),
        name="mem_attn",
    )(q, mk, mv)


def _merge_kernel(oa_ref, ob_ref, oc_ref, ga_ref, gb_ref, gc_ref, wa_ref, wb_ref, wc_ref, m_ref):
    m = ga_ref[...].astype(F32) * _dot(oa_ref[...], wa_ref[...])
    m = m + gb_ref[...].astype(F32) * _dot(ob_ref[...], wb_ref[...])
    m = m + gc_ref[...].astype(F32) * _dot(oc_ref[...], wc_ref[...])
    m_ref[...] = m.astype(m_ref.dtype)


def _merge(oa, ob, oc, gates, wa, wb, wc, *, tm=1024, tn=512):
    t = oa.shape[0]
    d = wa.shape[1]
    tm, tn = _tile(t, tm), _tile(d, tn)
    nj = d // tn
    row = lambda width: pl.BlockSpec((tm, width), lambda i, j: (i, 0))
    gate = lambda b: pl.BlockSpec((tm, tn), lambda i, j: (i, b * nj + j))
    wcol = lambda width: pl.BlockSpec((width, tn), lambda i, j: (0, j))
    return pl.pallas_call(
        _merge_kernel,
        out_shape=jax.ShapeDtypeStruct((t, d), BF16),
        grid=(t // tm, nj),
        in_specs=[row(oa.shape[1]), row(ob.shape[1]), row(oc.shape[1]), gate(0), gate(1), gate(2),
                  wcol(wa.shape[0]), wcol(wb.shape[0]), wcol(wc.shape[0])],
        out_specs=pl.BlockSpec((tm, tn), lambda i, j: (i, j)),
        compiler_params=_params("parallel", "arbitrary"),
        name="merge",
    )(oa, ob, oc, gates, gates, gates, wa, wb, wc)


def _layer_norm(x, g, b):
    mu = jnp.mean(x, axis=-1, keepdims=True)
    var = jnp.mean(jnp.square(x - mu), axis=-1, keepdims=True)
    return (x - mu) * lax.rsqrt(var + LN_EPS) * g + b


def _out_kernel(m_ref, w_ref, x_ref, g_ref, b_ref, h_ref, hb_ref, *, alpha):
    y = alpha * x_ref[...] + _dot(m_ref[...], w_ref[...])
    h = _layer_norm(y, g_ref[...], b_ref[...])
    h_ref[...] = h
    hb_ref[...] = h.astype(BF16)


def _out_proj(m, w_o, x, g, b, alpha, *, tm=512):
    t, d = x.shape
    tm = _tile(t, tm)
    row = pl.BlockSpec((tm, d), lambda i: (i, 0))
    vec = pl.BlockSpec((1, d), lambda i: (0, 0))
    return pl.pallas_call(
        functools.partial(_out_kernel, alpha=alpha),
        out_shape=(jax.ShapeDtypeStruct((t, d), F32), jax.ShapeDtypeStruct((t, d), BF16)),
        grid=(t // tm,),
        in_specs=[row, pl.BlockSpec((d, d), lambda i: (0, 0)), row, vec, vec],
        out_specs=(row, row),
        compiler_params=_params("parallel"),
        name="out_proj_ln1",
    )(m, w_o, x, g, b)


def _first_argmax(vals, idx, sentinel):
    m = jnp.max(vals, axis=0, keepdims=True)
    first = jnp.min(jnp.where(vals == m, idx, sentinel), axis=0, keepdims=True)
    return m, first


def _router_kernel(x_ref, w_ref, bias_ref, idx_ref, wt_ref):
    s = jax.nn.sigmoid(_dot(x_ref[...], w_ref[...]))
    st = s.T
    sb = st + bias_ref[...]
    n_exp, tm = st.shape
    gs = n_exp // N_GROUPS
    neg = -jnp.inf
    ridx = lax.broadcasted_iota(jnp.int32, (gs, tm), 0)
    grp = []
    for g in range(N_GROUPS):
        blk = sb[g * gs:(g + 1) * gs]
        m1, first = _first_argmax(blk, ridx, gs)
        m2 = jnp.max(jnp.where(ridx == first, neg, blk), axis=0, keepdims=True)
        grp.append(m1 + m2)
    work = jnp.concatenate(grp, axis=0)
    gidx = lax.broadcasted_iota(jnp.int32, work.shape, 0)
    gmask = jnp.zeros(work.shape, jnp.bool_)
    for _ in range(TOPK_GROUPS):
        _, first = _first_argmax(work, gidx, N_GROUPS)
        sel = gidx == first
        gmask = jnp.logical_or(gmask, sel)
        work = jnp.where(sel, neg, work)
    masked = jnp.concatenate(
        [jnp.where(gmask[g:g + 1], sb[g * gs:(g + 1) * gs], neg) for g in range(N_GROUPS)], axis=0)
    eidx = lax.broadcasted_iota(jnp.int32, masked.shape, 0)
    ids, wts = [], []
    for _ in range(TOP_K):
        _, first = _first_argmax(masked, eidx, n_exp)
        sel = eidx == first
        ids.append(first)
        wts.append(jnp.sum(jnp.where(sel, st, 0.0), axis=0, keepdims=True))
        masked = jnp.where(sel, neg, masked)
    wt = jnp.concatenate(wts, axis=0)
    idx_ref[...] = jnp.concatenate(ids, axis=0)
    wt_ref[...] = wt / jnp.sum(wt, axis=0, keepdims=True) * ROUTED_SCALE


def _router(hb, w_router, bias, *, tm=256):
    t, d = hb.shape
    n_exp = w_router.shape[1]
    tm = _tile(t, tm)
    return pl.pallas_call(
        _router_kernel,
        out_shape=(jax.ShapeDtypeStruct((TOP_K, t), jnp.int32), jax.ShapeDtypeStruct((TOP_K, t), F32)),
        grid=(t // tm,),
        in_specs=[pl.BlockSpec((tm, d), lambda i: (i, 0)),
                  pl.BlockSpec((d, n_exp), lambda i: (0, 0)),
                  pl.BlockSpec((n_exp, 1), lambda i: (0, 0))],
        out_specs=(pl.BlockSpec((TOP_K, tm), lambda i: (0, i)),
                   pl.BlockSpec((TOP_K, tm), lambda i: (0, i))),
        compiler_params=_params("parallel"),
        name="router",
    )(hb, w_router, bias)


def _expert_kernel(be_ref, nu_ref, x_ref, wg_ref, wu_ref, wd_ref, y_ref, w1_scr, wd_scr, *, f, up_off):
    i = pl.program_id(0)
    e = be_ref[i]
    prev = be_ref[jnp.maximum(i - 1, 0)]

    @pl.when(i == 0)
    def _():
        w1_scr[f:up_off, :] = jnp.zeros((up_off - f, w1_scr.shape[1]), BF16)

    @pl.when(jnp.logical_or(i == 0, e != prev))
    def _():
        w1_scr[0:f, :] = wg_ref[0].astype(BF16)
        w1_scr[up_off:up_off + f, :] = wu_ref[0].astype(BF16)
        wd_scr[...] = wd_ref[0].astype(BF16)

    @pl.when(i < nu_ref[0])
    def _():
        h = _dot_t(x_ref[...], w1_scr[...])
        a = jax.nn.silu(h[:, :f]) * h[:, up_off:up_off + f]
        y_ref[...] = _dot(a.astype(BF16), wd_scr[...]).astype(y_ref.dtype)

    @pl.when(i >= nu_ref[0])
    def _():
        y_ref[...] = jnp.zeros_like(y_ref)


def _experts(blk_e, n_used, x_sorted, w_gate_t, w_up_t, w_down):
    n_rows, d = x_sorted.shape
    f = w_down.shape[1]
    up_off = -(-f // LANES) * LANES
    n_blocks = n_rows // MOE_ROWS
    wspec = pl.BlockSpec((1, f, d), lambda i, be, nu: (be[i], 0, 0))
    grid_spec = pltpu.PrefetchScalarGridSpec(
        num_scalar_prefetch=2,
        grid=(n_blocks,),
        in_specs=[pl.BlockSpec((MOE_ROWS, d), lambda i, be, nu: (i, 0)), wspec, wspec, wspec],
        out_specs=pl.BlockSpec((MOE_ROWS, d), lambda i, be, nu: (i, 0)),
        scratch_shapes=[pltpu.VMEM((up_off + f, d), BF16), pltpu.VMEM((f, d), BF16)],
    )
    return pl.pallas_call(
        functools.partial(_expert_kernel, f=f, up_off=up_off),
        out_shape=jax.ShapeDtypeStruct((n_rows, d), BF16),
        grid_spec=grid_spec,
        compiler_params=_params("arbitrary"),
        name="experts",
    )(blk_e, n_used, x_sorted, w_gate_t, w_up_t, w_down)


def _final_kernel(h_ref, hb_ref, yg_ref, wt_ref, wg_ref, wu_ref, wd_ref, g_ref, b_ref, o_ref, *, alpha):
    hb = hb_ref[...]
    a = jax.nn.silu(_dot_t(hb, wg_ref[...])) * _dot_t(hb, wu_ref[...])
    shared = _dot(a.astype(BF16), wd_ref[...])
    wt = wt_ref[...]
    routed = wt[:, 0:1] * yg_ref[0].astype(F32)
    for k in range(1, yg_ref.shape[0]):
        routed = routed + wt[:, k:k + 1] * yg_ref[k].astype(F32)
    y = alpha * h_ref[...] + (routed + shared)
    o_ref[...] = _layer_norm(y, g_ref[...], b_ref[...])


def _final(h, hb, yg, wt, ws_gate_t, ws_up_t, ws_down, g, b, alpha, *, tm=256):
    t, d = h.shape
    f = ws_down.shape[0]
    tm = _tile(t, tm)
    row = pl.BlockSpec((tm, d), lambda i: (i, 0))
    vec = pl.BlockSpec((1, d), lambda i: (0, 0))
    wfull = pl.BlockSpec((f, d), lambda i: (0, 0))
    return pl.pallas_call(
        functools.partial(_final_kernel, alpha=alpha),
        out_shape=jax.ShapeDtypeStruct((t, d), F32),
        grid=(t // tm,),
        in_specs=[row, row, pl.BlockSpec((yg.shape[0], tm, d), lambda i: (0, i, 0)),
                  pl.BlockSpec((tm, wt.shape[1]), lambda i: (i, 0)), wfull, wfull, wfull, vec, vec],
        out_specs=row,
        compiler_params=_params("parallel"),
        name="shared_combine_ln2",
    )(h, hb, yg, wt, ws_gate_t, ws_up_t, ws_down, g, b)


def _dispatch(idx_t, n_exp):
    top_k, t = idx_t.shape
    tk = t * top_k
    onehot = (idx_t[:, :, None] == jnp.arange(n_exp, dtype=jnp.int32)).astype(jnp.int32)
    per_tok = jnp.sum(onehot, axis=0)
    cum = jnp.cumsum(per_tok, axis=0)
    counts = cum[-1]
    pcounts = (counts + MOE_ROWS - 1) // MOE_ROWS * MOE_ROWS
    pend = jnp.cumsum(pcounts)
    pstart = pend - pcounts
    dest = jnp.sum(onehot * (cum - per_tok + pstart)[None], axis=-1)
    n_blocks = -(-tk // MOE_ROWS) + n_exp
    blk_e = jnp.minimum(jnp.searchsorted(pend, jnp.arange(n_blocks) * MOE_ROWS, side="right"),
                        n_exp - 1).astype(jnp.int32)
    n_used = (pend[-1:] // MOE_ROWS).astype(jnp.int32)
    n_rows = n_blocks * MOE_ROWS
    assert n_rows == tk + n_exp * MOE_ROWS
    i = jnp.arange(MOE_ROWS, dtype=jnp.int32)[None, :]
    pad_keys = jnp.where(i < (pcounts - counts)[:, None], (pstart + counts)[:, None] + i, n_rows)
    keys = jnp.concatenate([dest.reshape(-1), pad_keys.reshape(-1)]).astype(jnp.int32)
    toks = jnp.concatenate([jnp.tile(jnp.arange(t, dtype=jnp.int32), top_k),
                            jnp.full((n_exp * MOE_ROWS,), t, jnp.int32)])
    _, row_tok = lax.sort((keys, toks), num_keys=1)
    return dest, row_tok, blk_e, n_used


def _moe(h, hb, w_router, router_bias, w_gate_t, w_up_t, w_down, ws_gate_t, ws_up_t, ws_down, g, b, alpha):
    t, d = h.shape
    n_exp = w_router.shape[1]
    idx_t, wt_t = _router(hb, w_router, router_bias.reshape(n_exp, 1))
    dest, row_tok, blk_e, n_used = _dispatch(idx_t, n_exp)
    x_pad = jnp.concatenate([hb, jnp.zeros((1, d), hb.dtype)], axis=0)
    x_sorted = x_pad[row_tok]
    y_sorted = _experts(blk_e, n_used, x_sorted, w_gate_t, w_up_t, w_down)
    yg = y_sorted[dest]
    return _final(h, hb, yg, wt_t.T, ws_gate_t, ws_up_t, ws_down, g, b, alpha)


def _rope_tables(pos):
    half = B_HEAD_DIM // 2
    inv = 1.0 / (ROPE_THETA ** (jnp.arange(half, dtype=F32) / half))
    ang = pos.astype(F32)[:, None] * inv[None, :]
    cos, sin = jnp.cos(ang), jnp.sin(ang)
    reps = LANES // B_HEAD_DIM
    return (jnp.concatenate([cos, cos] * reps, axis=1), jnp.concatenate([-sin, sin] * reps, axis=1))


def kernel(x_prompt, x_sample, mem_prompt, cache_k, cache_v, page_table, cache_mem_k, cache_mem_v, w_in, a_ln_g, a_ln_b, a_ws, a_bs, lam_q1, lam_k1, lam_q2, lam_k2, b_subln_g, w_mk, w_mv, w_pa, w_pb, w_pc, w_o, ln1_g, ln1_b, w_router, router_bias, w_gate, w_up, w_down, ws_gate, ws_up, ws_down, ln2_g, ln2_b):
    depth = w_in.shape[0]
    assert depth == 1, "single-layer step"
    layer = 0
    batch, seq, d = x_prompt.shape
    n_seq, dec, _ = x_sample.shape
    n_mem = mem_prompt.shape[1]
    n_phys, page, n_bh = cache_k.shape[1:4]
    past = page_table.shape[1] * page
    c_width = cache_mem_k.shape[3] * C_HEAD_DIM
    a_width = A_GROUPS * A_GROUP_DIM
    qk_width = n_bh * 2 * B_HEAD_DIM
    bv_width = n_bh * B_V_DIM
    assert CHUNK % dec == 0 and seq % CHUNK == 0
    alpha = (2.0 * depth) ** 0.25
    lam_init = _lambda_init(layer)
    tp, ts = batch * seq, n_seq * dec
    t = tp + ts

    x = jnp.concatenate([x_prompt.reshape(tp, d), x_sample.reshape(ts, d)], axis=0)
    xb = x.astype(BF16)
    w_in_b = w_in[layer].astype(BF16)
    o = [0, 2 * a_width]
    for wdt in (qk_width, qk_width, bv_width, c_width, 3 * d):
        o.append(o[-1] + wdt)
    pos = jnp.concatenate([jnp.tile(jnp.arange(seq), batch), past + jnp.tile(jnp.arange(dec), n_seq)])
    rope = _rope_tables(pos)

    reps = CHUNK // dec
    ws = a_ws[layer]
    ws_s = jnp.einsum("ab,gts->gatbs", jnp.eye(reps, dtype=ws.dtype), ws[:, :dec, :dec]).reshape(
        A_GROUPS, CHUNK, CHUNK)
    ws2 = jnp.stack([ws, ws_s]).astype(BF16)
    bs = a_bs[layer]
    bs2 = jnp.stack([bs, jnp.tile(bs[:, :dec], (1, reps))])
    bs2 = jnp.broadcast_to(bs2[..., None], bs2.shape + (CHUNK,)).astype(F32)

    o_a, vn = _proj_a(xb, w_in_b[:, o[0]:o[1]], a_ln_g[layer].reshape(1, a_width),
                      a_ln_b[layer].reshape(1, a_width), ws2, bs2, tp)
    q = _mm(xb, w_in_b[:, o[1]:o[2]], BF16, epilogue="rope", scale=B_HEAD_DIM ** -0.5, rope=rope)
    v = _mm(xb, w_in_b[:, o[3]:o[4]], F32)
    cq = _mm(xb, w_in_b[:, o[4]:o[5]], BF16)
    gates = _mm(xb, w_in_b[:, o[5]:o[6]], BF16, epilogue="sigmoid")
    w_k = w_in_b[:, o[2]:o[3]]
    w_k_t = w_k.T
    cos_t, sin_t = rope[0][:, :B_HEAD_DIM // 2].T, rope[1][:, B_HEAD_DIM // 2:B_HEAD_DIM].T
    kt_p = _proj_kt(w_k_t, xb, cos_t, sin_t, batch, tp, tm=512)
    xs_t = jnp.swapaxes(x_sample, 0, 1).reshape(ts, d).astype(BF16)
    pos_s = past + jnp.repeat(jnp.arange(dec), n_seq)
    cos_s, sin_s = _rope_tables(pos_s)
    kt_s = _proj_kt(w_k_t, xs_t, cos_s[:, :B_HEAD_DIM // 2].T, sin_s[:, B_HEAD_DIM // 2:B_HEAD_DIM].T, dec,
                    ts, tm=n_seq)
    k_s = _mm(xb[tp:], w_k, F32, epilogue="rope", rope=(rope[0][tp:], rope[1][tp:]))

    lamv = jnp.stack([lam_q1[layer], lam_k1[layer], lam_q2[layer], lam_k2[layer]]).astype(F32)
    g_sub = b_subln_g[layer].reshape(1, B_V_DIM)
    ob_p = _attn_prompt(lamv, q, kt_p, v, g_sub, batch, seq, n_bh, lam_init)
    cache_kt = jnp.transpose(cache_k[layer], (0, 2, 3, 4, 1)).reshape(n_phys, n_bh, 2 * B_HEAD_DIM, page)
    ob_s = _attn_sample(page_table, lamv, q[tp:].reshape(n_seq, dec, qk_width),
                        k_s.reshape(n_seq, dec, qk_width), v[tp:].reshape(n_seq, dec, bv_width), g_sub,
                        cache_kt, cache_v[layer].reshape(n_phys, page * n_bh, B_V_DIM), lam_init)
    o_b = jnp.concatenate([ob_p, ob_s.reshape(ts, bv_width)], axis=0)

    c_heads = c_width // C_HEAD_DIM
    memb = mem_prompt.reshape(batch * n_mem, d).astype(BF16)
    mk = _mm(memb, w_mk[layer].astype(BF16), F32)
    mv = _mm(memb, w_mv[layer].astype(BF16), F32)
    oc_p = _mem_attn(cq[:tp].reshape(batch, seq, c_width), mk.reshape(batch, n_mem, c_width),
                     mv.reshape(batch, n_mem, c_width), interleaved=False)
    oc_s = _mem_attn(cq[tp:].reshape(n_seq, dec, c_width),
                     cache_mem_k[layer].reshape(n_seq, n_mem * c_heads, C_HEAD_DIM),
                     cache_mem_v[layer].reshape(n_seq, n_mem * c_heads, C_HEAD_DIM), interleaved=True)
    o_c = jnp.concatenate([oc_p.reshape(tp, c_width), oc_s.reshape(ts, c_width)], axis=0)

    m = _merge(o_a, o_b, o_c, gates, w_pa[layer].astype(BF16), w_pb[layer].astype(BF16),
               w_pc[layer].astype(BF16))
    h1, h1b = _out_proj(m, w_o[layer].astype(BF16), x, ln1_g[layer].reshape(1, d), ln1_b[layer].reshape(1, d),
                        alpha)
    y = _moe(h1, h1b, w_router[layer].astype(BF16), router_bias[layer],
             jnp.swapaxes(w_gate[layer], 1, 2), jnp.swapaxes(w_up[layer], 1, 2), w_down[layer],
             ws_gate[layer].T.astype(BF16), ws_up[layer].T.astype(BF16), ws_down[layer].astype(BF16),
             ln2_g[layer].reshape(1, d), ln2_b[layer].reshape(1, d), alpha)

    k_prompt = jnp.transpose(kt_p.reshape(batch, n_bh, 2, B_HEAD_DIM, seq), (0, 4, 1, 2, 3))
    k_sample = jnp.transpose(kt_s.reshape(dec, n_bh, 2, B_HEAD_DIM, n_seq), (4, 0, 1, 2, 3))
    return (y[:tp].reshape(batch, seq, d),
            y[tp:].reshape(n_seq, dec, d),
            k_prompt[None],
            v[:tp].reshape(1, batch, seq, n_bh, B_V_DIM),
            mk.reshape(1, batch, n_mem, c_heads, C_HEAD_DIM),
            mv.reshape(1, batch, n_mem, c_heads, C_HEAD_DIM),
            k_sample[None],
            v[tp:].reshape(1, n_seq, dec, n_bh, B_V_DIM),
            vn[tp:].reshape(1, n_seq, dec, a_width))
```

```python
import functools
import math

import jax
import jax.numpy as jnp
from jax import lax
from jax.experimental import pallas as pl
from jax.experimental.pallas import tpu as pltpu

F32 = jnp.float32
BF16 = jnp.bfloat16

LN_EPS = 1e-5
ROPE_THETA = 10000.0
LANES = 128
A_GROUP_DIM = 128
A_GROUPS = 4
CHUNK = 128
B_HEAD_DIM = 64
B_V_DIM = 128
C_HEAD_DIM = 128
N_GROUPS = 8
TOPK_GROUPS = 4
TOP_K = 8
ROUTED_SCALE = 2.5
MOE_ROWS = 256
VMEM_LIMIT = 56 * 1024 * 1024


def _lambda_init(layer):
    return 0.8 - 0.6 * math.exp(-0.3 * layer)


def _params(*sem):
    return pltpu.CompilerParams(dimension_semantics=sem, vmem_limit_bytes=VMEM_LIMIT)


def _tile(n, pref):
    t = min(n, pref)
    while n % t:
        t //= 2
    return t


def _dot(a, b):
    return jnp.dot(a, b, preferred_element_type=F32)


def _dot_t(a, b):
    return lax.dot_general(a, b, (((1,), (1,)), ((), ())), preferred_element_type=F32)


def _mm_kernel(*refs, epilogue, scale):
    if epilogue == "rope":
        x_ref, w_ref, cos_ref, sin_ref, o_ref = refs
    else:
        x_ref, w_ref, o_ref = refs
    acc = _dot(x_ref[...], w_ref[...])
    if epilogue == "rope":
        n = acc.shape[1] // LANES
        cos = jnp.concatenate([cos_ref[...]] * n, axis=1)
        sin = jnp.concatenate([sin_ref[...]] * n, axis=1)
        lane = lax.broadcasted_iota(jnp.int32, acc.shape, 1)
        first = (lane % B_HEAD_DIM) < (B_HEAD_DIM // 2)
        half = B_HEAD_DIM // 2
        swapped = jnp.where(first, pltpu.roll(acc, acc.shape[1] - half, 1), pltpu.roll(acc, half, 1))
        acc = acc * cos + swapped * sin
    elif epilogue == "sigmoid":
        acc = jax.nn.sigmoid(acc)
    if scale != 1.0:
        acc = acc * scale
    o_ref[...] = acc.astype(o_ref.dtype)


def _mm(x, w, out_dtype, *, epilogue="none", scale=1.0, rope=None, tm=512, tn=512):
    m, k = x.shape
    n = w.shape[1]
    tm, tn = _tile(m, tm), _tile(n, tn)
    in_specs = [pl.BlockSpec((tm, k), lambda i, j: (i, 0)),
                pl.BlockSpec((k, tn), lambda i, j: (0, j))]
    args = [x, w]
    if epilogue == "rope":
        in_specs += [pl.BlockSpec((tm, LANES), lambda i, j: (i, 0))] * 2
        args += list(rope)
    return pl.pallas_call(
        functools.partial(_mm_kernel, epilogue=epilogue, scale=scale),
        out_shape=jax.ShapeDtypeStruct((m, n), out_dtype),
        grid=(m // tm, n // tn),
        in_specs=in_specs,
        out_specs=pl.BlockSpec((tm, tn), lambda i, j: (i, j)),
        compiler_params=_params("parallel", "arbitrary"),
        name="proj_" + epilogue,
    )(*args)


def _kt_kernel(w_ref, x_ref, cos_ref, sin_ref, o_ref):
    acc = _dot_t(w_ref[...], x_ref[...])
    cos, sin = cos_ref[...], sin_ref[...]
    half = B_HEAD_DIM // 2
    for g in range(acc.shape[0] // B_HEAD_DIM):
        lo = g * B_HEAD_DIM
        x1 = acc[lo:lo + half]
        x2 = acc[lo + half:lo + B_HEAD_DIM]
        o_ref[0, lo:lo + half, :] = x1 * cos - x2 * sin
        o_ref[0, lo + half:lo + B_HEAD_DIM, :] = x2 * cos + x1 * sin


def _proj_kt(w_t, x, cos_t, sin_t, n_groups, m, *, tm):
    n, k = w_t.shape
    per = m // n_groups
    tm = _tile(per, tm)
    nt = per // tm
    half = B_HEAD_DIM // 2
    return pl.pallas_call(
        _kt_kernel,
        out_shape=jax.ShapeDtypeStruct((n_groups, n, per), F32),
        grid=(m // tm,),
        in_specs=[pl.BlockSpec((n, k), lambda i: (0, 0)),
                  pl.BlockSpec((tm, k), lambda i: (i, 0)),
                  pl.BlockSpec((half, tm), lambda i: (0, i)),
                  pl.BlockSpec((half, tm), lambda i: (0, i))],
        out_specs=pl.BlockSpec((1, n, tm), lambda i: (i // nt, 0, i % nt)),
        compiler_params=_params("parallel"),
        name="proj_kt",
    )(w_t, x, cos_t, sin_t)


def _proj_a_kernel(x_ref, w_ref, lng_ref, lnb_ref, ws_ref, bs_ref, oa_ref, vn_ref):
    h = _dot(x_ref[...], w_ref[...])
    aw = h.shape[1] // 2
    u = jax.nn.gelu(h[:, :aw])
    g = jax.nn.gelu(h[:, aw:])
    mu = jnp.mean(g, axis=-1, keepdims=True)
    var = jnp.mean(jnp.square(g - mu), axis=-1, keepdims=True)
    vn = (g - mu) * lax.rsqrt(var + LN_EPS) * lng_ref[...] + lnb_ref[...]
    vn_ref[...] = vn
    vnb = vn.astype(BF16)
    row = lax.broadcasted_iota(jnp.int32, (CHUNK, CHUNK), 0)
    col = lax.broadcasted_iota(jnp.int32, (CHUNK, CHUNK), 1)
    causal = col <= row
    for gi in range(A_GROUPS):
        wsg = jnp.where(causal, ws_ref[0, gi], jnp.zeros((), BF16))
        cols = slice(gi * A_GROUP_DIM, (gi + 1) * A_GROUP_DIM)
        for c in range(h.shape[0] // CHUNK):
            rows = slice(c * CHUNK, (c + 1) * CHUNK)
            s = _dot(wsg, vnb[rows, cols]) + bs_ref[0, gi]
            oa_ref[rows, cols] = (u[rows, cols] * s).astype(oa_ref.dtype)


def _proj_a(x, w, lng, lnb, ws2, bs2, n_prompt, *, tm=512):
    m, k = x.shape
    n = w.shape[1]
    tm = _tile(math.gcd(m, n_prompt), tm)
    npt = n_prompt // tm
    sel = lambda i: (jnp.where(i >= npt, 1, 0), 0, 0, 0)
    return pl.pallas_call(
        _proj_a_kernel,
        out_shape=(jax.ShapeDtypeStruct((m, n // 2), BF16), jax.ShapeDtypeStruct((m, n // 2), F32)),
        grid=(m // tm,),
        in_specs=[pl.BlockSpec((tm, k), lambda i: (i, 0)),
                  pl.BlockSpec((k, n), lambda i: (0, 0)),
                  pl.BlockSpec((1, n // 2), lambda i: (0, 0)),
                  pl.BlockSpec((1, n // 2), lambda i: (0, 0)),
                  pl.BlockSpec((1, A_GROUPS, CHUNK, CHUNK), sel),
                  pl.BlockSpec((1, A_GROUPS, CHUNK, CHUNK), sel)],
        out_specs=(pl.BlockSpec((tm, n // 2), lambda i: (i, 0)),
                   pl.BlockSpec((tm, n // 2), lambda i: (i, 0))),
        compiler_params=_params("parallel"),
        name="proj_a",
    )(x, w, lng, lnb, ws2, bs2)


def _diff_lambda(lam_ref, lam_init):
    lv = lam_ref[...]
    s1 = jnp.sum(lv[0:1] * lv[1:2], axis=-1, keepdims=True)
    s2 = jnp.sum(lv[2:3] * lv[3:4], axis=-1, keepdims=True)
    return jnp.exp(s1) - jnp.exp(s2) + lam_init


def _stack_maps(q):
    lane = lax.broadcasted_iota(jnp.int32, q.shape, 1)
    zero = jnp.zeros((), q.dtype)
    return jnp.concatenate([jnp.where(lane < B_HEAD_DIM, q, zero),
                            jnp.where(lane >= B_HEAD_DIM, q, zero)], axis=0)


def _sub_norm(o, g, lam_init):
    r = lax.rsqrt(jnp.mean(o * o, axis=-1, keepdims=True) + LN_EPS)
    return o * r * g * (1.0 - lam_init)


def _attn_p_kernel(lam_ref, q_ref, kt_ref, v_ref, g_ref, o_ref, kt_scr, vb_scr, *, tq, nh, lam_init):
    qi = pl.program_id(2)
    hd = B_V_DIM

    @pl.when(qi == 0)
    def _():
        for kb in range(kt_scr.shape[0]):
            kt_scr[kb] = kt_ref[0, :, kb * tq:(kb + 1) * tq].astype(BF16)
        vb_scr[...] = v_ref[...].astype(BF16)

    qs = [_stack_maps(q_ref[:, h * hd:(h + 1) * hd]) for h in range(nh)]

    def step(carry, kb, diag):
        off = pl.multiple_of(kb * tq, tq)
        kt = kt_scr[kb]
        out = []
        for h in range(nh):
            s = _dot(qs[h], kt[h * hd:(h + 1) * hd])
            if diag:
                row = lax.broadcasted_iota(jnp.int32, s.shape, 0) % tq
                col = lax.broadcasted_iota(jnp.int32, s.shape, 1)
                s = jnp.where(col <= row, s, -jnp.inf)
            m, l, acc = carry[h]
            m_new = jnp.maximum(m, jnp.max(s, axis=-1, keepdims=True))
            a = jnp.exp(m - m_new)
            p = jnp.exp(s - m_new)
            l = a * l + jnp.sum(p, axis=-1, keepdims=True)
            acc = a * acc + _dot(p.astype(BF16), vb_scr[pl.ds(off, tq), h * hd:(h + 1) * hd])
            out.append((m_new, l, acc))
        return tuple(out)

    init = tuple((jnp.full((2 * tq, 1), -jnp.inf, F32), jnp.zeros((2 * tq, 1), F32),
                  jnp.zeros((2 * tq, hd), F32)) for _ in range(nh))
    carry = lax.fori_loop(0, qi, lambda kb, c: step(c, kb, False), init)
    carry = step(carry, qi, True)
    lam = _diff_lambda(lam_ref, lam_init)
    for h in range(nh):
        _, l, acc = carry[h]
        o = acc / l
        o = o[:tq] - lam * o[tq:]
        o_ref[:, h * hd:(h + 1) * hd] = _sub_norm(o, g_ref[...], lam_init).astype(o_ref.dtype)


def _attn_prompt(lamv, q, kt, v, g, batch, seq, n_heads, lam_init, *, tq=256, nh=2):
    tq = _tile(seq, tq)
    nq = seq // tq
    nh = math.gcd(nh, n_heads)
    w = nh * B_V_DIM
    return pl.pallas_call(
        functools.partial(_attn_p_kernel, tq=tq, nh=nh, lam_init=lam_init),
        out_shape=jax.ShapeDtypeStruct((batch * seq, n_heads * B_V_DIM), BF16),
        grid=(batch, n_heads // nh, nq),
        in_specs=[pl.BlockSpec(lamv.shape, lambda b, h, i: (0, 0)),
                  pl.BlockSpec((tq, w), lambda b, h, i: (b * nq + i, h)),
                  pl.BlockSpec((1, w, seq), lambda b, h, i: (b, h, 0)),
                  pl.BlockSpec((seq, w), lambda b, h, i: (b, h)),
                  pl.BlockSpec((1, B_V_DIM), lambda b, h, i: (0, 0))],
        out_specs=pl.BlockSpec((tq, w), lambda b, h, i: (b * nq + i, h)),
        scratch_shapes=[pltpu.VMEM((nq, w, tq), BF16), pltpu.VMEM((seq, w), BF16)],
        compiler_params=_params("parallel", "parallel", "arbitrary"),
        name="attn_prompt",
    )(lamv, q, kt, v, g)


def _attn_s_kernel(pt_ref, lam_ref, q_ref, kn_ref, vn_ref, g_ref, *refs, n_pages, n_heads, page, dec,
                   lam_init):
    kp = refs[:n_pages]
    vp = refs[n_pages:2 * n_pages]
    o_ref = refs[2 * n_pages]
    s_scr, kn_scr, vn_scr = refs[2 * n_pages + 1:]

    @pl.when(pl.program_id(0) == 0)
    def _():
        kn_scr[...] = jnp.zeros_like(kn_scr)
        vn_scr[...] = jnp.zeros_like(vn_scr)

    kn_scr[0:dec, :] = kn_ref[0]
    vn_scr[0:dec, :] = vn_ref[0]
    lam = _diff_lambda(lam_ref, lam_init)
    past = n_pages * page
    for h in range(n_heads):
        cols = slice(h * B_V_DIM, (h + 1) * B_V_DIM)
        qs = _stack_maps(q_ref[0, :, cols])
        for p in range(n_pages):
            s_scr[:, p * page:(p + 1) * page] = _dot(qs, kp[p][0, h].astype(BF16))
        sn = _dot_t(qs, kn_scr[:, cols].astype(BF16))
        row = lax.broadcasted_iota(jnp.int32, sn.shape, 0) % dec
        col = lax.broadcasted_iota(jnp.int32, sn.shape, 1)
        s_scr[:, past:] = jnp.where(col <= row, sn, -jnp.inf)
        s = s_scr[...]
        m = jnp.max(s, axis=-1, keepdims=True)
        e = jnp.exp(s - m)
        pr = (e / jnp.sum(e, axis=-1, keepdims=True)).astype(BF16)
        o = _dot(pr[:, past:], vn_scr[:, cols].astype(BF16))
        for p in range(n_pages):
            vh = vp[p][0, pl.ds(h, page, stride=n_heads), :]
            o = o + _dot(pr[:, p * page:(p + 1) * page], vh.astype(BF16))
        o = o[:dec] - lam * o[dec:]
        o_ref[0, :, cols] = _sub_norm(o, g_ref[...], lam_init).astype(o_ref.dtype)


def _attn_sample(page_table, lamv, q, k_new, v_new, g, cache_kt, cache_v, lam_init):
    n_seq, dec, width = q.shape
    n_pages = page_table.shape[1]
    n_heads = width // B_V_DIM
    page = cache_kt.shape[3]
    tok = lambda b, pt: (b, 0, 0)
    const = lambda b, pt: (0, 0)

    def k_spec(p):
        return pl.BlockSpec((1, n_heads, B_V_DIM, page), lambda b, pt: (pt[b, p], 0, 0, 0))

    def v_spec(p):
        return pl.BlockSpec((1, page * n_heads, B_V_DIM), lambda b, pt: (pt[b, p], 0, 0))

    grid_spec = pltpu.PrefetchScalarGridSpec(
        num_scalar_prefetch=1,
        grid=(n_seq,),
        in_specs=[pl.BlockSpec(lamv.shape, const),
                  pl.BlockSpec((1, dec, width), tok),
                  pl.BlockSpec((1, dec, width), tok),
                  pl.BlockSpec((1, dec, width), tok),
                  pl.BlockSpec((1, B_V_DIM), const)]
                 + [k_spec(p) for p in range(n_pages)]
                 + [v_spec(p) for p in range(n_pages)],
        out_specs=pl.BlockSpec((1, dec, width), tok),
        scratch_shapes=[pltpu.VMEM((2 * dec, (n_pages + 1) * page), F32),
                        pltpu.VMEM((page, width), F32),
                        pltpu.VMEM((page, width), F32)],
    )
    return pl.pallas_call(
        functools.partial(_attn_s_kernel, n_pages=n_pages, n_heads=n_heads, page=page, dec=dec,
                          lam_init=lam_init),
        out_shape=jax.ShapeDtypeStruct((n_seq, dec, width), BF16),
        grid_spec=grid_spec,
        compiler_params=_params("arbitrary"),
        name="attn_sample",
    )(page_table, lamv, q, k_new, v_new, g, *([cache_kt] * n_pages), *([cache_v] * n_pages))


def _mem_attn_kernel(q_ref, k_ref, v_ref, o_ref, *, n_heads, interleaved):
    tq = q_ref.shape[1]
    pad = (-tq) % 16
    n_mem = k_ref.shape[1] // n_heads if interleaved else k_ref.shape[1]
    for h in range(n_heads):
        cols = slice(h * C_HEAD_DIM, (h + 1) * C_HEAD_DIM)
        if interleaved:
            kh = k_ref[0, pl.ds(h, n_mem, stride=n_heads), :]
            vh = v_ref[0, pl.ds(h, n_mem, stride=n_heads), :]
        else:
            kh, vh = k_ref[0, :, cols], v_ref[0, :, cols]
        q = q_ref[0, :, cols]
        if pad:
            q = jnp.concatenate([q, jnp.zeros((pad, C_HEAD_DIM), q.dtype)], axis=0)
        s = _dot_t(q, kh.astype(BF16)) * (C_HEAD_DIM ** -0.5)
        m = jnp.max(s, axis=-1, keepdims=True)
        e = jnp.exp(s - m)
        pr = (e / jnp.sum(e, axis=-1, keepdims=True)).astype(BF16)
        o = _dot(pr, vh.astype(BF16))
        o_ref[0, :, cols] = o[:tq].astype(o_ref.dtype)


def _mem_attn(q, mk, mv, *, interleaved, tq=512):
    b, t, w = q.shape
    tq = _tile(t, tq)
    kv_spec = pl.BlockSpec((1,) + mk.shape[1:], lambda i, j: (i, 0, 0))
    return pl.pallas_call(
        functools.partial(_mem_attn_kernel, n_heads=w // C_HEAD_DIM, interleaved=interleaved),
        out_shape=jax.ShapeDtypeStruct((b, t, w), BF16),
        grid=(b, t // tq),
        in_specs=[pl.BlockSpec((1, tq, w), lambda i, j: (i, j, 0)), kv_spec, kv_spec],
        out_specs=pl.BlockSpec((1, tq, w), lambda i, j: (i, j, 0)),
        compiler_params=_params("parallel", "arbitrary"),
        name="mem_attn",
    )(q, mk, mv)


def _merge_kernel(oa_ref, ob_ref, oc_ref, ga_ref, gb_ref, gc_ref, wa_ref, wb_ref, wc_ref, m_ref):
    m = ga_ref[...].astype(F32) * _dot(oa_ref[...], wa_ref[...])
    m = m + gb_ref[...].astype(F32) * _dot(ob_ref[...], wb_ref[...])
    m = m + gc_ref[...].astype(F32) * _dot(oc_ref[...], wc_ref[...])
    m_ref[...] = m.astype(m_ref.dtype)


def _merge(oa, ob, oc, gates, wa, wb, wc, *, tm=1024, tn=512):
    t = oa.shape[0]
    d = wa.shape[1]
    tm, tn = _tile(t, tm), _tile(d, tn)
    nj = d // tn
    row = lambda width: pl.BlockSpec((tm, width), lambda i, j: (i, 0))
    gate = lambda b: pl.BlockSpec((tm, tn), lambda i, j: (i, b * nj + j))
    wcol = lambda width: pl.BlockSpec((width, tn), lambda i, j: (0, j))
    return pl.pallas_call(
        _merge_kernel,
        out_shape=jax.ShapeDtypeStruct((t, d), BF16),
        grid=(t // tm, nj),
        in_specs=[row(oa.shape[1]), row(ob.shape[1]), row(oc.shape[1]), gate(0), gate(1), gate(2),
                  wcol(wa.shape[0]), wcol(wb.shape[0]), wcol(wc.shape[0])],
        out_specs=pl.BlockSpec((tm, tn), lambda i, j: (i, j)),
        compiler_params=_params("parallel", "arbitrary"),
        name="merge",
    )(oa, ob, oc, gates, gates, gates, wa, wb, wc)


def _layer_norm(x, g, b):
    mu = jnp.mean(x, axis=-1, keepdims=True)
    var = jnp.mean(jnp.square(x - mu), axis=-1, keepdims=True)
    return (x - mu) * lax.rsqrt(var + LN_EPS) * g + b


def _out_kernel(m_ref, w_ref, x_ref, g_ref, b_ref, h_ref, hb_ref, *, alpha):
    y = alpha * x_ref[...] + _dot(m_ref[...], w_ref[...])
    h = _layer_norm(y, g_ref[...], b_ref[...])
    h_ref[...] = h
    hb_ref[...] = h.astype(BF16)


def _out_proj(m, w_o, x, g, b, alpha, *, tm=512):
    t, d = x.shape
    tm = _tile(t, tm)
    row = pl.BlockSpec((tm, d), lambda i: (i, 0))
    vec = pl.BlockSpec((1, d), lambda i: (0, 0))
    return pl.pallas_call(
        functools.partial(_out_kernel, alpha=alpha),
        out_shape=(jax.ShapeDtypeStruct((t, d), F32), jax.ShapeDtypeStruct((t, d), BF16)),
        grid=(t // tm,),
        in_specs=[row, pl.BlockSpec((d, d), lambda i: (0, 0)), row, vec, vec],
        out_specs=(row, row),
        compiler_params=_params("parallel"),
        name="out_proj_ln1",
    )(m, w_o, x, g, b)


def _first_argmax(vals, idx, sentinel):
    m = jnp.max(vals, axis=0, keepdims=True)
    first = jnp.min(jnp.where(vals == m, idx, sentinel), axis=0, keepdims=True)
    return m, first


def _router_kernel(x_ref, w_ref, bias_ref, idx_ref, wt_ref):
    s = jax.nn.sigmoid(_dot(x_ref[...], w_ref[...]))
    st = s.T
    sb = st + bias_ref[...]
    n_exp, tm = st.shape
    gs = n_exp // N_GROUPS
    neg = -jnp.inf
    ridx = lax.broadcasted_iota(jnp.int32, (gs, tm), 0)
    grp = []
    for g in range(N_GROUPS):
        blk = sb[g * gs:(g + 1) * gs]
        m1, first = _first_argmax(blk, ridx, gs)
        m2 = jnp.max(jnp.where(ridx == first, neg, blk), axis=0, keepdims=True)
        grp.append(m1 + m2)
    work = jnp.concatenate(grp, axis=0)
    gidx = lax.broadcasted_iota(jnp.int32, work.shape, 0)
    gmask = jnp.zeros(work.shape, jnp.bool_)
    for _ in range(TOPK_GROUPS):
        _, first = _first_argmax(work, gidx, N_GROUPS)
        sel = gidx == first
        gmask = jnp.logical_or(gmask, sel)
        work = jnp.where(sel, neg, work)
    masked = jnp.concatenate(
        [jnp.where(gmask[g:g + 1], sb[g * gs:(g + 1) * gs], neg) for g in range(N_GROUPS)], axis=0)
    eidx = lax.broadcasted_iota(jnp.int32, masked.shape, 0)
    ids, wts = [], []
    for _ in range(TOP_K):
        _, first = _first_argmax(masked, eidx, n_exp)
        sel = eidx == first
        ids.append(first)
        wts.append(jnp.sum(jnp.where(sel, st, 0.0), axis=0, keepdims=True))
        masked = jnp.where(sel, neg, masked)
    wt = jnp.concatenate(wts, axis=0)
    idx_ref[...] = jnp.concatenate(ids, axis=0)
    wt_ref[...] = wt / jnp.sum(wt, axis=0, keepdims=True) * ROUTED_SCALE


def _router(hb, w_router, bias, *, tm=256):
    t, d = hb.shape
    n_exp = w_router.shape[1]
    tm = _tile(t, tm)
    return pl.pallas_call(
        _router_kernel,
        out_shape=(jax.ShapeDtypeStruct((TOP_K, t), jnp.int32), jax.ShapeDtypeStruct((TOP_K, t), F32)),
        grid=(t // tm,),
        in_specs=[pl.BlockSpec((tm, d), lambda i: (i, 0)),
                  pl.BlockSpec((d, n_exp), lambda i: (0, 0)),
                  pl.BlockSpec((n_exp, 1), lambda i: (0, 0))],
        out_specs=(pl.BlockSpec((TOP_K, tm), lambda i: (0, i)),
                   pl.BlockSpec((TOP_K, tm), lambda i: (0, i))),
        compiler_params=_params("parallel"),
        name="router",
    )(hb, w_router, bias)


def _expert_kernel(b0_ref, nb_ref, nu_ref, x_hbm, wg_ref, wu_ref, wd_ref, y_hbm, xbuf, ybuf, sem_in, sem_out,
                   w1_scr, wd_scr, *, f, up_off, n_blocks):
    e = pl.program_id(0)
    b0 = b0_ref[e]
    nb = nb_ref[e]

    def x_copy(blk, slot):
        return pltpu.make_async_copy(x_hbm.at[pl.ds(blk * MOE_ROWS, MOE_ROWS)], xbuf.at[slot], sem_in.at[slot])

    def y_copy(blk, slot):
        return pltpu.make_async_copy(ybuf.at[slot], y_hbm.at[pl.ds(blk * MOE_ROWS, MOE_ROWS)], sem_out.at[slot])

    @pl.when(e == 0)
    def _():
        w1_scr[f:up_off, :] = jnp.zeros((up_off - f, w1_scr.shape[1]), BF16)

    @pl.when(nb > 0)
    def _():
        x_copy(b0, 0).start()
        w1_scr[0:f, :] = wg_ref[0].astype(BF16)
        w1_scr[up_off:up_off + f, :] = wu_ref[0].astype(BF16)
        wd_scr[...] = wd_ref[0].astype(BF16)

        def chunk(c, carry):
            slot = c % 2
            x_copy(b0 + c, slot).wait()

            @pl.when(c + 1 < nb)
            def _():
                x_copy(b0 + c + 1, 1 - slot).start()

            @pl.when(c >= 2)
            def _():
                y_copy(b0 + c - 2, slot).wait()

            h = _dot_t(xbuf[slot], w1_scr[...])
            a = jax.nn.silu(h[:, :f]) * h[:, up_off:up_off + f]
            ybuf[slot] = _dot(a.astype(BF16), wd_scr[...]).astype(ybuf.dtype)
            y_copy(b0 + c, slot).start()
            return carry

        lax.fori_loop(0, nb, chunk, 0)

        @pl.when(nb >= 2)
        def _():
            y_copy(b0 + nb - 2, nb % 2).wait()

        y_copy(b0 + nb - 1, (nb - 1) % 2).wait()

    @pl.when(e == pl.num_programs(0) - 1)
    def _():
        ybuf[0] = jnp.zeros(ybuf.shape[1:], ybuf.dtype)

        def fill(blk, carry):
            y_copy(blk, 0).start()
            y_copy(blk, 0).wait()
            return carry

        lax.fori_loop(nu_ref[0], n_blocks, fill, 0)


def _experts(blk0, nblk, n_used, x_sorted, w_gate_t, w_up_t, w_down):
    n_rows, d = x_sorted.shape
    n_exp, f, _ = w_down.shape
    up_off = -(-f // LANES) * LANES
    n_blocks = n_rows // MOE_ROWS
    wspec = pl.BlockSpec((1, f, d), lambda e, b0, nb, nu: (e, 0, 0))
    grid_spec = pltpu.PrefetchScalarGridSpec(
        num_scalar_prefetch=3,
        grid=(n_exp,),
        in_specs=[pl.BlockSpec(memory_space=pl.ANY), wspec, wspec, wspec],
        out_specs=pl.BlockSpec(memory_space=pl.ANY),
        scratch_shapes=[pltpu.VMEM((2, MOE_ROWS, d), BF16), pltpu.VMEM((2, MOE_ROWS, d), BF16),
                        pltpu.SemaphoreType.DMA((2,)), pltpu.SemaphoreType.DMA((2,)),
                        pltpu.VMEM((up_off + f, d), BF16), pltpu.VMEM((f, d), BF16)],
    )
    return pl.pallas_call(
        functools.partial(_expert_kernel, f=f, up_off=up_off, n_blocks=n_blocks),
        out_shape=jax.ShapeDtypeStruct((n_rows, d), BF16),
        grid_spec=grid_spec,
        compiler_params=_params("arbitrary"),
        name="experts",
    )(blk0, nblk, n_used, x_sorted, w_gate_t, w_up_t, w_down)


def _final_kernel(h_ref, hb_ref, yg_ref, wt_ref, wg_ref, wu_ref, wd_ref, g_ref, b_ref, o_ref, *, alpha):
    hb = hb_ref[...]
    a = jax.nn.silu(_dot_t(hb, wg_ref[...])) * _dot_t(hb, wu_ref[...])
    shared = _dot(a.astype(BF16), wd_ref[...])
    wt = wt_ref[...]
    routed = wt[:, 0:1] * yg_ref[0].astype(F32)
    for k in range(1, yg_ref.shape[0]):
        routed = routed + wt[:, k:k + 1] * yg_ref[k].astype(F32)
    y = alpha * h_ref[...] + (routed + shared)
    o_ref[...] = _layer_norm(y, g_ref[...], b_ref[...])


def _final(h, hb, yg, wt, ws_gate_t, ws_up_t, ws_down, g, b, alpha, *, tm=256):
    t, d = h.shape
    f = ws_down.shape[0]
    tm = _tile(t, tm)
    row = pl.BlockSpec((tm, d), lambda i: (i, 0))
    vec = pl.BlockSpec((1, d), lambda i: (0, 0))
    wfull = pl.BlockSpec((f, d), lambda i: (0, 0))
    return pl.pallas_call(
        functools.partial(_final_kernel, alpha=alpha),
        out_shape=jax.ShapeDtypeStruct((t, d), F32),
        grid=(t // tm,),
        in_specs=[row, row, pl.BlockSpec((yg.shape[0], tm, d), lambda i: (0, i, 0)),
                  pl.BlockSpec((tm, wt.shape[1]), lambda i: (i, 0)), wfull, wfull, wfull, vec, vec],
        out_specs=row,
        compiler_params=_params("parallel"),
        name="shared_combine_ln2",
    )(h, hb, yg, wt, ws_gate_t, ws_up_t, ws_down, g, b)


def _dispatch(idx_t, n_exp):
    top_k, t = idx_t.shape
    tk = t * top_k
    onehot = (idx_t[:, :, None] == jnp.arange(n_exp, dtype=jnp.int32)).astype(jnp.int32)
    per_tok = jnp.sum(onehot, axis=0)
    cum = jnp.cumsum(per_tok, axis=0)
    counts = cum[-1]
    pcounts = (counts + MOE_ROWS - 1) // MOE_ROWS * MOE_ROWS
    pend = jnp.cumsum(pcounts)
    pstart = pend - pcounts
    dest = jnp.sum(onehot * (cum - per_tok + pstart)[None], axis=-1)
    n_blocks = -(-tk // MOE_ROWS) + n_exp
    n_used = (pend[-1:] // MOE_ROWS).astype(jnp.int32)
    n_rows = n_blocks * MOE_ROWS
    assert n_rows == tk + n_exp * MOE_ROWS
    i = jnp.arange(MOE_ROWS, dtype=jnp.int32)[None, :]
    pad_keys = jnp.where(i < (pcounts - counts)[:, None], (pstart + counts)[:, None] + i, n_rows)
    keys = jnp.concatenate([dest.reshape(-1), pad_keys.reshape(-1)]).astype(jnp.int32)
    toks = jnp.concatenate([jnp.tile(jnp.arange(t, dtype=jnp.int32), top_k),
                            jnp.arange(n_exp * MOE_ROWS, dtype=jnp.int32) % t])
    _, row_tok = lax.sort((keys, toks), num_keys=1)
    return dest, row_tok, (pstart // MOE_ROWS).astype(jnp.int32), (pcounts // MOE_ROWS).astype(jnp.int32), n_used


def _moe(h, hb, w_router, router_bias, w_gate_t, w_up_t, w_down, ws_gate_t, ws_up_t, ws_down, g, b, alpha):
    t, d = h.shape
    n_exp = w_router.shape[1]
    idx_t, wt_t = _router(hb, w_router, router_bias.reshape(n_exp, 1))
    dest, row_tok, blk0, nblk, n_used = _dispatch(idx_t, n_exp)
    x_sorted = hb[row_tok]
    y_sorted = _experts(blk0, nblk, n_used, x_sorted, w_gate_t, w_up_t, w_down)
    yg = y_sorted[dest]
    return _final(h, hb, yg, wt_t.T, ws_gate_t, ws_up_t, ws_down, g, b, alpha)


def _rope_tables(pos):
    half = B_HEAD_DIM // 2
    inv = 1.0 / (ROPE_THETA ** (jnp.arange(half, dtype=F32) / half))
    ang = pos.astype(F32)[:, None] * inv[None, :]
    cos, sin = jnp.cos(ang), jnp.sin(ang)
    reps = LANES // B_HEAD_DIM
    return (jnp.concatenate([cos, cos] * reps, axis=1), jnp.concatenate([-sin, sin] * reps, axis=1))


def kernel(x_prompt, x_sample, mem_prompt, cache_k, cache_v, page_table, cache_mem_k, cache_mem_v, w_in, a_ln_g, a_ln_b, a_ws, a_bs, lam_q1, lam_k1, lam_q2, lam_k2, b_subln_g, w_mk, w_mv, w_pa, w_pb, w_pc, w_o, ln1_g, ln1_b, w_router, router_bias, w_gate, w_up, w_down, ws_gate, ws_up, ws_down, ln2_g, ln2_b):
    depth = w_in.shape[0]
    assert depth == 1, "single-layer step"
    layer = 0
    batch, seq, d = x_prompt.shape
    n_seq, dec, _ = x_sample.shape
    n_mem = mem_prompt.shape[1]
    n_phys, page, n_bh = cache_k.shape[1:4]
    past = page_table.shape[1] * page
    c_width = cache_mem_k.shape[3] * C_HEAD_DIM
    a_width = A_GROUPS * A_GROUP_DIM
    qk_width = n_bh * 2 * B_HEAD_DIM
    bv_width = n_bh * B_V_DIM
    assert CHUNK % dec == 0 and seq % CHUNK == 0
    alpha = (2.0 * depth) ** 0.25
    lam_init = _lambda_init(layer)
    tp, ts = batch * seq, n_seq * dec
    t = tp + ts

    x = jnp.concatenate([x_prompt.reshape(tp, d), x_sample.reshape(ts, d)], axis=0)
    xb = x.astype(BF16)
    w_in_b = w_in[layer].astype(BF16)
    o = [0, 2 * a_width]
    for wdt in (qk_width, qk_width, bv_width, c_width, 3 * d):
        o.append(o[-1] + wdt)
    pos = jnp.concatenate([jnp.tile(jnp.arange(seq), batch), past + jnp.tile(jnp.arange(dec), n_seq)])
    rope = _rope_tables(pos)

    reps = CHUNK // dec
    ws = a_ws[layer]
    ws_s = jnp.einsum("ab,gts->gatbs", jnp.eye(reps, dtype=ws.dtype), ws[:, :dec, :dec]).reshape(
        A_GROUPS, CHUNK, CHUNK)
    ws2 = jnp.stack([ws, ws_s]).astype(BF16)
    bs = a_bs[layer]
    bs2 = jnp.stack([bs, jnp.tile(bs[:, :dec], (1, reps))])
    bs2 = jnp.broadcast_to(bs2[..., None], bs2.shape + (CHUNK,)).astype(F32)

    o_a, vn = _proj_a(xb, w_in_b[:, o[0]:o[1]], a_ln_g[layer].reshape(1, a_width),
                      a_ln_b[layer].reshape(1, a_width), ws2, bs2, tp)
    q = _mm(xb, w_in_b[:, o[1]:o[2]], BF16, epilogue="rope", scale=B_HEAD_DIM ** -0.5, rope=rope)
    v = _mm(xb, w_in_b[:, o[3]:o[4]], F32)
    cq = _mm(xb, w_in_b[:, o[4]:o[5]], BF16)
    gates = _mm(xb, w_in_b[:, o[5]:o[6]], BF16, epilogue="sigmoid")
    w_k = w_in_b[:, o[2]:o[3]]
    w_k_t = w_k.T
    cos_t, sin_t = rope[0][:, :B_HEAD_DIM // 2].T, rope[1][:, B_HEAD_DIM // 2:B_HEAD_DIM].T
    kt_p = _proj_kt(w_k_t, xb, cos_t, sin_t, batch, tp, tm=512)
    xs_t = jnp.swapaxes(x_sample, 0, 1).reshape(ts, d).astype(BF16)
    pos_s = past + jnp.repeat(jnp.arange(dec), n_seq)
    cos_s, sin_s = _rope_tables(pos_s)
    kt_s = _proj_kt(w_k_t, xs_t, cos_s[:, :B_HEAD_DIM // 2].T, sin_s[:, B_HEAD_DIM // 2:B_HEAD_DIM].T, dec,
                    ts, tm=n_seq)
    k_s = _mm(xb[tp:], w_k, F32, epilogue="rope", rope=(rope[0][tp:], rope[1][tp:]))

    lamv = jnp.stack([lam_q1[layer], lam_k1[layer], lam_q2[layer], lam_k2[layer]]).astype(F32)
    g_sub = b_subln_g[layer].reshape(1, B_V_DIM)
    ob_p = _attn_prompt(lamv, q, kt_p, v, g_sub, batch, seq, n_bh, lam_init)
    cache_kt = jnp.transpose(cache_k[layer], (0, 2, 3, 4, 1)).reshape(n_phys, n_bh, 2 * B_HEAD_DIM, page)
    ob_s = _attn_sample(page_table, lamv, q[tp:].reshape(n_seq, dec, qk_width),
                        k_s.reshape(n_seq, dec, qk_width), v[tp:].reshape(n_seq, dec, bv_width), g_sub,
                        cache_kt, cache_v[layer].reshape(n_phys, page * n_bh, B_V_DIM), lam_init)
    o_b = jnp.concatenate([ob_p, ob_s.reshape(ts, bv_width)], axis=0)

    c_heads = c_width // C_HEAD_DIM
    memb = mem_prompt.reshape(batch * n_mem, d).astype(BF16)
    mk = _mm(memb, w_mk[layer].astype(BF16), F32)
    mv = _mm(memb, w_mv[layer].astype(BF16), F32)
    oc_p = _mem_attn(cq[:tp].reshape(batch, seq, c_width), mk.reshape(batch, n_mem, c_width),
                     mv.reshape(batch, n_mem, c_width), interleaved=False)
    oc_s = _mem_attn(cq[tp:].reshape(n_seq, dec, c_width),
                     cache_mem_k[layer].reshape(n_seq, n_mem * c_heads, C_HEAD_DIM),
                     cache_mem_v[layer].reshape(n_seq, n_mem * c_heads, C_HEAD_DIM), interleaved=True)
    o_c = jnp.concatenate([oc_p.reshape(tp, c_width), oc_s.reshape(ts, c_width)], axis=0)

    m = _merge(o_a, o_b, o_c, gates, w_pa[layer].astype(BF16), w_pb[layer].astype(BF16),
               w_pc[layer].astype(BF16))
    h1, h1b = _out_proj(m, w_o[layer].astype(BF16), x, ln1_g[layer].reshape(1, d), ln1_b[layer].reshape(1, d),
                        alpha)
    y = _moe(h1, h1b, w_router[layer].astype(BF16), router_bias[layer],
             jnp.swapaxes(w_gate[layer], 1, 2), jnp.swapaxes(w_up[layer], 1, 2), w_down[layer],
             ws_gate[layer].T.astype(BF16), ws_up[layer].T.astype(BF16), ws_down[layer].astype(BF16),
             ln2_g[layer].reshape(1, d), ln2_b[layer].reshape(1, d), alpha)

    k_prompt = jnp.transpose(kt_p.reshape(batch, n_bh, 2, B_HEAD_DIM, seq), (0, 4, 1, 2, 3))
    k_sample = jnp.transpose(kt_s.reshape(dec, n_bh, 2, B_HEAD_DIM, n_seq), (4, 0, 1, 2, 3))
    return (y[:tp].reshape(batch, seq, d),
            y[tp:].reshape(n_seq, dec, d),
            k_prompt[None],
            v[:tp].reshape(1, batch, seq, n_bh, B_V_DIM),
            mk.reshape(1, batch, n_mem, c_heads, C_HEAD_DIM),
            mv.reshape(1, batch, n_mem, c_heads, C_HEAD_DIM),
            k_sample[None],
            v[tp:].reshape(1, n_seq, dec, n_bh, B_V_DIM),
            vn[tp:].reshape(1, n_seq, dec, a_width))
```

```python
import functools
import math

import jax
import jax.numpy as jnp
from jax import lax
from jax.experimental import pallas as pl
from jax.experimental.pallas import tpu as pltpu

F32 = jnp.float32
BF16 = jnp.bfloat16

LN_EPS = 1e-5
ROPE_THETA = 10000.0
LANES = 128
A_GROUP_DIM = 128
A_GROUPS = 4
CHUNK = 128
B_HEAD_DIM = 64
B_V_DIM = 128
C_HEAD_DIM = 128
N_GROUPS = 8
TOPK_GROUPS = 4
TOP_K = 8
ROUTED_SCALE = 2.5
MOE_ROWS = 256
VMEM_LIMIT = 56 * 1024 * 1024


def _lambda_init(layer):
    return 0.8 - 0.6 * math.exp(-0.3 * layer)


def _params(*sem):
    return pltpu.CompilerParams(dimension_semantics=sem, vmem_limit_bytes=VMEM_LIMIT)


def _tile(n, pref):
    t = min(n, pref)
    while n % t:
        t //= 2
    return t


def _dot(a, b):
    return jnp.dot(a, b, preferred_element_type=F32)


def _dot_t(a, b):
    return lax.dot_general(a, b, (((1,), (1,)), ((), ())), preferred_element_type=F32)


def _mm_kernel(*refs, epilogue, scale):
    if epilogue == "rope":
        x_ref, w_ref, cos_ref, sin_ref, o_ref = refs
    else:
        x_ref, w_ref, o_ref = refs
    acc = _dot(x_ref[...], w_ref[...])
    if epilogue == "rope":
        n = acc.shape[1] // LANES
        cos = jnp.concatenate([cos_ref[...]] * n, axis=1)
        sin = jnp.concatenate([sin_ref[...]] * n, axis=1)
        lane = lax.broadcasted_iota(jnp.int32, acc.shape, 1)
        first = (lane % B_HEAD_DIM) < (B_HEAD_DIM // 2)
        half = B_HEAD_DIM // 2
        swapped = jnp.where(first, pltpu.roll(acc, acc.shape[1] - half, 1), pltpu.roll(acc, half, 1))
        acc = acc * cos + swapped * sin
    elif epilogue == "sigmoid":
        acc = jax.nn.sigmoid(acc)
    if scale != 1.0:
        acc = acc * scale
    o_ref[...] = acc.astype(o_ref.dtype)


def _mm(x, w, out_dtype, *, epilogue="none", scale=1.0, rope=None, tn=1024):
    m, k = x.shape
    n = w.shape[1]
    tm, tn = _tile(m, 512 if epilogue == "rope" else 1024), _tile(n, tn)
    in_specs = [pl.BlockSpec((tm, k), lambda i, j: (i, 0)),
                pl.BlockSpec((k, tn), lambda i, j: (0, j))]
    args = [x, w]
    if epilogue == "rope":
        in_specs += [pl.BlockSpec((tm, LANES), lambda i, j: (i, 0))] * 2
        args += list(rope)
    return pl.pallas_call(
        functools.partial(_mm_kernel, epilogue=epilogue, scale=scale),
        out_shape=jax.ShapeDtypeStruct((m, n), out_dtype),
        grid=(m // tm, n // tn),
        in_specs=in_specs,
        out_specs=pl.BlockSpec((tm, tn), lambda i, j: (i, j)),
        compiler_params=_params("parallel", "arbitrary"),
        name="proj_" + epilogue,
    )(*args)


def _kt_kernel(w_ref, x_ref, cos_ref, sin_ref, o_ref):
    acc = _dot_t(w_ref[...], x_ref[...])
    cos, sin = cos_ref[...], sin_ref[...]
    half = B_HEAD_DIM // 2
    for g in range(acc.shape[0] // B_HEAD_DIM):
        lo = g * B_HEAD_DIM
        x1 = acc[lo:lo + half]
        x2 = acc[lo + half:lo + B_HEAD_DIM]
        o_ref[0, lo:lo + half, :] = x1 * cos - x2 * sin
        o_ref[0, lo + half:lo + B_HEAD_DIM, :] = x2 * cos + x1 * sin


def _proj_kt(w_t, x, cos_t, sin_t, n_groups, m, *, tm):
    n, k = w_t.shape
    per = m // n_groups
    tm = _tile(per, tm)
    nt = per // tm
    half = B_HEAD_DIM // 2
    return pl.pallas_call(
        _kt_kernel,
        out_shape=jax.ShapeDtypeStruct((n_groups, n, per), F32),
        grid=(m // tm,),
        in_specs=[pl.BlockSpec((n, k), lambda i: (0, 0)),
                  pl.BlockSpec((tm, k), lambda i: (i, 0)),
                  pl.BlockSpec((half, tm), lambda i: (0, i)),
                  pl.BlockSpec((half, tm), lambda i: (0, i))],
        out_specs=pl.BlockSpec((1, n, tm), lambda i: (i // nt, 0, i % nt)),
        compiler_params=_params("parallel"),
        name="proj_kt",
    )(w_t, x, cos_t, sin_t)


def _proj_a_kernel(x_ref, w_ref, lng_ref, lnb_ref, ws_ref, bs_ref, oa_ref, vn_ref):
    h = _dot(x_ref[...], w_ref[...])
    aw = h.shape[1] // 2
    u = jax.nn.gelu(h[:, :aw])
    g = jax.nn.gelu(h[:, aw:])
    mu = jnp.mean(g, axis=-1, keepdims=True)
    var = jnp.mean(jnp.square(g - mu), axis=-1, keepdims=True)
    vn = (g - mu) * lax.rsqrt(var + LN_EPS) * lng_ref[...] + lnb_ref[...]
    vn_ref[...] = vn
    vnb = vn.astype(BF16)
    row = lax.broadcasted_iota(jnp.int32, (CHUNK, CHUNK), 0)
    col = lax.broadcasted_iota(jnp.int32, (CHUNK, CHUNK), 1)
    causal = col <= row
    for gi in range(A_GROUPS):
        wsg = jnp.where(causal, ws_ref[0, gi], jnp.zeros((), BF16))
        cols = slice(gi * A_GROUP_DIM, (gi + 1) * A_GROUP_DIM)
        for c in range(h.shape[0] // CHUNK):
            rows = slice(c * CHUNK, (c + 1) * CHUNK)
            s = _dot(wsg, vnb[rows, cols]) + bs_ref[0, gi]
            oa_ref[rows, cols] = (u[rows, cols] * s).astype(oa_ref.dtype)


def _proj_a(x, w, lng, lnb, ws2, bs2, n_prompt, *, tm=512):
    m, k = x.shape
    n = w.shape[1]
    tm = _tile(math.gcd(m, n_prompt), tm)
    npt = n_prompt // tm
    sel = lambda i: (jnp.where(i >= npt, 1, 0), 0, 0, 0)
    return pl.pallas_call(
        _proj_a_kernel,
        out_shape=(jax.ShapeDtypeStruct((m, n // 2), BF16), jax.ShapeDtypeStruct((m, n // 2), F32)),
        grid=(m // tm,),
        in_specs=[pl.BlockSpec((tm, k), lambda i: (i, 0)),
                  pl.BlockSpec((k, n), lambda i: (0, 0)),
                  pl.BlockSpec((1, n // 2), lambda i: (0, 0)),
                  pl.BlockSpec((1, n // 2), lambda i: (0, 0)),
                  pl.BlockSpec((1, A_GROUPS, CHUNK, CHUNK), sel),
                  pl.BlockSpec((1, A_GROUPS, CHUNK, CHUNK), sel)],
        out_specs=(pl.BlockSpec((tm, n // 2), lambda i: (i, 0)),
                   pl.BlockSpec((tm, n // 2), lambda i: (i, 0))),
        compiler_params=_params("parallel"),
        name="proj_a",
    )(x, w, lng, lnb, ws2, bs2)


def _diff_lambda(lam_ref, lam_init):
    lv = lam_ref[...]
    s1 = jnp.sum(lv[0:1] * lv[1:2], axis=-1, keepdims=True)
    s2 = jnp.sum(lv[2:3] * lv[3:4], axis=-1, keepdims=True)
    return jnp.exp(s1) - jnp.exp(s2) + lam_init


def _stack_maps(q):
    lane = lax.broadcasted_iota(jnp.int32, q.shape, 1)
    zero = jnp.zeros((), q.dtype)
    return jnp.concatenate([jnp.where(lane < B_HEAD_DIM, q, zero),
                            jnp.where(lane >= B_HEAD_DIM, q, zero)], axis=0)


def _sub_norm(o, g, lam_init):
    r = lax.rsqrt(jnp.mean(o * o, axis=-1, keepdims=True) + LN_EPS)
    return o * r * g * (1.0 - lam_init)


def _attn_p_kernel(lam_ref, q_ref, kt_ref, v_ref, g_ref, o_ref, kt_scr, vb_scr, *, tq, nh, lam_init):
    qi = pl.program_id(2)
    hd = B_V_DIM

    @pl.when(qi == 0)
    def _():
        for kb in range(kt_scr.shape[0]):
            kt_scr[kb] = kt_ref[0, :, kb * tq:(kb + 1) * tq].astype(BF16)
        vb_scr[...] = v_ref[...].astype(BF16)

    qs = [_stack_maps(q_ref[:, h * hd:(h + 1) * hd]) for h in range(nh)]

    def step(carry, kb, diag):
        off = pl.multiple_of(kb * tq, tq)
        kt = kt_scr[kb]
        out = []
        for h in range(nh):
            s = _dot(qs[h], kt[h * hd:(h + 1) * hd])
            if diag:
                row = lax.broadcasted_iota(jnp.int32, s.shape, 0) % tq
                col = lax.broadcasted_iota(jnp.int32, s.shape, 1)
                s = jnp.where(col <= row, s, -jnp.inf)
            m, l, acc = carry[h]
            m_new = jnp.maximum(m, jnp.max(s, axis=-1, keepdims=True))
            a = jnp.exp(m - m_new)
            p = jnp.exp(s - m_new)
            l = a * l + jnp.sum(p, axis=-1, keepdims=True)
            acc = a * acc + _dot(p.astype(BF16), vb_scr[pl.ds(off, tq), h * hd:(h + 1) * hd])
            out.append((m_new, l, acc))
        return tuple(out)

    init = tuple((jnp.full((2 * tq, 1), -jnp.inf, F32), jnp.zeros((2 * tq, 1), F32),
                  jnp.zeros((2 * tq, hd), F32)) for _ in range(nh))
    carry = lax.fori_loop(0, qi, lambda kb, c: step(c, kb, False), init)
    carry = step(carry, qi, True)
    lam = _diff_lambda(lam_ref, lam_init)
    for h in range(nh):
        _, l, acc = carry[h]
        o = acc / l
        o = o[:tq] - lam * o[tq:]
        o_ref[:, h * hd:(h + 1) * hd] = _sub_norm(o, g_ref[...], lam_init).astype(o_ref.dtype)


def _attn_prompt(lamv, q, kt, v, g, batch, seq, n_heads, lam_init, *, tq=256, nh=2):
    tq = _tile(seq, tq)
    nq = seq // tq
    nh = math.gcd(nh, n_heads)
    w = nh * B_V_DIM
    return pl.pallas_call(
        functools.partial(_attn_p_kernel, tq=tq, nh=nh, lam_init=lam_init),
        out_shape=jax.ShapeDtypeStruct((batch * seq, n_heads * B_V_DIM), BF16),
        grid=(batch, n_heads // nh, nq),
        in_specs=[pl.BlockSpec(lamv.shape, lambda b, h, i: (0, 0)),
                  pl.BlockSpec((tq, w), lambda b, h, i: (b * nq + i, h)),
                  pl.BlockSpec((1, w, seq), lambda b, h, i: (b, h, 0)),
                  pl.BlockSpec((seq, w), lambda b, h, i: (b, h)),
                  pl.BlockSpec((1, B_V_DIM), lambda b, h, i: (0, 0))],
        out_specs=pl.BlockSpec((tq, w), lambda b, h, i: (b * nq + i, h)),
        scratch_shapes=[pltpu.VMEM((nq, w, tq), BF16), pltpu.VMEM((seq, w), BF16)],
        compiler_params=_params("parallel", "parallel", "arbitrary"),
        name="attn_prompt",
    )(lamv, q, kt, v, g)


def _attn_s_kernel(pt_ref, lam_ref, q_ref, kn_ref, vn_ref, g_ref, *refs, n_pages, n_heads, page, dec,
                   lam_init):
    kp = refs[:n_pages]
    vp = refs[n_pages:2 * n_pages]
    o_ref = refs[2 * n_pages]
    s_scr, kn_scr, vn_scr = refs[2 * n_pages + 1:]

    @pl.when(pl.program_id(0) == 0)
    def _():
        kn_scr[...] = jnp.zeros_like(kn_scr)
        vn_scr[...] = jnp.zeros_like(vn_scr)

    kn_scr[0:dec, :] = kn_ref[0]
    vn_scr[0:dec, :] = vn_ref[0]
    lam = _diff_lambda(lam_ref, lam_init)
    past = n_pages * page
    for h in range(n_heads):
        cols = slice(h * B_V_DIM, (h + 1) * B_V_DIM)
        qs = _stack_maps(q_ref[0, :, cols])
        for p in range(n_pages):
            s_scr[:, p * page:(p + 1) * page] = _dot(qs, kp[p][0, h].astype(BF16))
        sn = _dot_t(qs, kn_scr[:, cols].astype(BF16))
        row = lax.broadcasted_iota(jnp.int32, sn.shape, 0) % dec
        col = lax.broadcasted_iota(jnp.int32, sn.shape, 1)
        s_scr[:, past:] = jnp.where(col <= row, sn, -jnp.inf)
        s = s_scr[...]
        m = jnp.max(s, axis=-1, keepdims=True)
        e = jnp.exp(s - m)
        pr = (e / jnp.sum(e, axis=-1, keepdims=True)).astype(BF16)
        o = _dot(pr[:, past:], vn_scr[:, cols].astype(BF16))
        for p in range(n_pages):
            vh = vp[p][0, pl.ds(h, page, stride=n_heads), :]
            o = o + _dot(pr[:, p * page:(p + 1) * page], vh.astype(BF16))
        o = o[:dec] - lam * o[dec:]
        o_ref[0, :, cols] = _sub_norm(o, g_ref[...], lam_init).astype(o_ref.dtype)


def _attn_sample(page_table, lamv, q, k_new, v_new, g, cache_kt, cache_v, lam_init):
    n_seq, dec, width = q.shape
    n_pages = page_table.shape[1]
    n_heads = width // B_V_DIM
    page = cache_kt.shape[3]
    tok = lambda b, pt: (b, 0, 0)
    const = lambda b, pt: (0, 0)

    def k_spec(p):
        return pl.BlockSpec((1, n_heads, B_V_DIM, page), lambda b, pt: (pt[b, p], 0, 0, 0))

    def v_spec(p):
        return pl.BlockSpec((1, page * n_heads, B_V_DIM), lambda b, pt: (pt[b, p], 0, 0))

    grid_spec = pltpu.PrefetchScalarGridSpec(
        num_scalar_prefetch=1,
        grid=(n_seq,),
        in_specs=[pl.BlockSpec(lamv.shape, const),
                  pl.BlockSpec((1, dec, width), tok),
                  pl.BlockSpec((1, dec, width), tok),
                  pl.BlockSpec((1, dec, width), tok),
                  pl.BlockSpec((1, B_V_DIM), const)]
                 + [k_spec(p) for p in range(n_pages)]
                 + [v_spec(p) for p in range(n_pages)],
        out_specs=pl.BlockSpec((1, dec, width), tok),
        scratch_shapes=[pltpu.VMEM((2 * dec, (n_pages + 1) * page), F32),
                        pltpu.VMEM((page, width), F32),
                        pltpu.VMEM((page, width), F32)],
    )
    return pl.pallas_call(
        functools.partial(_attn_s_kernel, n_pages=n_pages, n_heads=n_heads, page=page, dec=dec,
                          lam_init=lam_init),
        out_shape=jax.ShapeDtypeStruct((n_seq, dec, width), BF16),
        grid_spec=grid_spec,
        compiler_params=_params("arbitrary"),
        name="attn_sample",
    )(page_table, lamv, q, k_new, v_new, g, *([cache_kt] * n_pages), *([cache_v] * n_pages))


def _mem_attn_kernel(q_ref, k_ref, v_ref, o_ref, *, n_heads, interleaved):
    tq = q_ref.shape[1]
    pad = (-tq) % 16
    n_mem = k_ref.shape[1] // n_heads if interleaved else k_ref.shape[1]
    for h in range(n_heads):
        cols = slice(h * C_HEAD_DIM, (h + 1) * C_HEAD_DIM)
        if interleaved:
            kh = k_ref[0, pl.ds(h, n_mem, stride=n_heads), :]
            vh = v_ref[0, pl.ds(h, n_mem, stride=n_heads), :]
        else:
            kh, vh = k_ref[0, :, cols], v_ref[0, :, cols]
        q = q_ref[0, :, cols]
        if pad:
            q = jnp.concatenate([q, jnp.zeros((pad, C_HEAD_DIM), q.dtype)], axis=0)
        s = _dot_t(q, kh.astype(BF16)) * (C_HEAD_DIM ** -0.5)
        m = jnp.max(s, axis=-1, keepdims=True)
        e = jnp.exp(s - m)
        pr = (e / jnp.sum(e, axis=-1, keepdims=True)).astype(BF16)
        o = _dot(pr, vh.astype(BF16))
        o_ref[0, :, cols] = o[:tq].astype(o_ref.dtype)


def _mem_attn(q, mk, mv, *, interleaved, tq=512):
    b, t, w = q.shape
    tq = _tile(t, tq)
    kv_spec = pl.BlockSpec((1,) + mk.shape[1:], lambda i, j: (i, 0, 0))
    return pl.pallas_call(
        functools.partial(_mem_attn_kernel, n_heads=w // C_HEAD_DIM, interleaved=interleaved),
        out_shape=jax.ShapeDtypeStruct((b, t, w), BF16),
        grid=(b, t // tq),
        in_specs=[pl.BlockSpec((1, tq, w), lambda i, j: (i, j, 0)), kv_spec, kv_spec],
        out_specs=pl.BlockSpec((1, tq, w), lambda i, j: (i, j, 0)),
        compiler_params=_params("parallel", "arbitrary"),
        name="mem_attn",
    )(q, mk, mv)


def _merge_kernel(oa_ref, ob_ref, oc_ref, ga_ref, gb_ref, gc_ref, wa_ref, wb_ref, wc_ref, m_ref):
    m = ga_ref[...].astype(F32) * _dot(oa_ref[...], wa_ref[...])
    m = m + gb_ref[...].astype(F32) * _dot(ob_ref[...], wb_ref[...])
    m = m + gc_ref[...].astype(F32) * _dot(oc_ref[...], wc_ref[...])
    m_ref[...] = m.astype(m_ref.dtype)


def _merge(oa, ob, oc, gates, wa, wb, wc, *, tm=1024, tn=512):
    t = oa.shape[0]
    d = wa.shape[1]
    tm, tn = _tile(t, tm), _tile(d, tn)
    nj = d // tn
    row = lambda width: pl.BlockSpec((tm, width), lambda i, j: (i, 0))
    gate = lambda b: pl.BlockSpec((tm, tn), lambda i, j: (i, b * nj + j))
    wcol = lambda width: pl.BlockSpec((width, tn), lambda i, j: (0, j))
    return pl.pallas_call(
        _merge_kernel,
        out_shape=jax.ShapeDtypeStruct((t, d), BF16),
        grid=(t // tm, nj),
        in_specs=[row(oa.shape[1]), row(ob.shape[1]), row(oc.shape[1]), gate(0), gate(1), gate(2),
                  wcol(wa.shape[0]), wcol(wb.shape[0]), wcol(wc.shape[0])],
        out_specs=pl.BlockSpec((tm, tn), lambda i, j: (i, j)),
        compiler_params=_params("parallel", "arbitrary"),
        name="merge",
    )(oa, ob, oc, gates, gates, gates, wa, wb, wc)


def _layer_norm(x, g, b):
    mu = jnp.mean(x, axis=-1, keepdims=True)
    var = jnp.mean(jnp.square(x - mu), axis=-1, keepdims=True)
    return (x - mu) * lax.rsqrt(var + LN_EPS) * g + b


def _out_kernel(m_ref, w_ref, x_ref, g_ref, b_ref, h_ref, hb_ref, *, alpha):
    y = alpha * x_ref[...] + _dot(m_ref[...], w_ref[...])
    h = _layer_norm(y, g_ref[...], b_ref[...])
    h_ref[...] = h
    hb_ref[...] = h.astype(BF16)


def _out_proj(m, w_o, x, g, b, alpha, *, tm=512):
    t, d = x.shape
    tm = _tile(t, tm)
    row = pl.BlockSpec((tm, d), lambda i: (i, 0))
    vec = pl.BlockSpec((1, d), lambda i: (0, 0))
    return pl.pallas_call(
        functools.partial(_out_kernel, alpha=alpha),
        out_shape=(jax.ShapeDtypeStruct((t, d), F32), jax.ShapeDtypeStruct((t, d), BF16)),
        grid=(t // tm,),
        in_specs=[row, pl.BlockSpec((d, d), lambda i: (0, 0)), row, vec, vec],
        out_specs=(row, row),
        compiler_params=_params("parallel"),
        name="out_proj_ln1",
    )(m, w_o, x, g, b)


def _first_argmax(vals, idx, sentinel):
    m = jnp.max(vals, axis=0, keepdims=True)
    first = jnp.min(jnp.where(vals == m, idx, sentinel), axis=0, keepdims=True)
    return m, first


def _router_kernel(x_ref, w_ref, bias_ref, idx_ref, wt_ref):
    s = jax.nn.sigmoid(_dot(x_ref[...], w_ref[...]))
    st = s.T
    sb = st + bias_ref[...]
    n_exp, tm = st.shape
    gs = n_exp // N_GROUPS
    neg = -jnp.inf
    ridx = lax.broadcasted_iota(jnp.int32, (gs, tm), 0)
    grp = []
    for g in range(N_GROUPS):
        blk = sb[g * gs:(g + 1) * gs]
        m1, first = _first_argmax(blk, ridx, gs)
        m2 = jnp.max(jnp.where(ridx == first, neg, blk), axis=0, keepdims=True)
        grp.append(m1 + m2)
    work = jnp.concatenate(grp, axis=0)
    gidx = lax.broadcasted_iota(jnp.int32, work.shape, 0)
    gmask = jnp.zeros(work.shape, jnp.bool_)
    for _ in range(TOPK_GROUPS):
        _, first = _first_argmax(work, gidx, N_GROUPS)
        sel = gidx == first
        gmask = jnp.logical_or(gmask, sel)
        work = jnp.where(sel, neg, work)
    masked = jnp.concatenate(
        [jnp.where(gmask[g:g + 1], sb[g * gs:(g + 1) * gs], neg) for g in range(N_GROUPS)], axis=0)
    eidx = lax.broadcasted_iota(jnp.int32, masked.shape, 0)
    ids, wts = [], []
    for _ in range(TOP_K):
        _, first = _first_argmax(masked, eidx, n_exp)
        sel = eidx == first
        ids.append(first)
        wts.append(jnp.sum(jnp.where(sel, st, 0.0), axis=0, keepdims=True))
        masked = jnp.where(sel, neg, masked)
    wt = jnp.concatenate(wts, axis=0)
    idx_ref[...] = jnp.concatenate(ids, axis=0)
    wt_ref[...] = wt / jnp.sum(wt, axis=0, keepdims=True) * ROUTED_SCALE


def _router(hb, w_router, bias, *, tm=256):
    t, d = hb.shape
    n_exp = w_router.shape[1]
    tm = _tile(t, tm)
    return pl.pallas_call(
        _router_kernel,
        out_shape=(jax.ShapeDtypeStruct((TOP_K, t), jnp.int32), jax.ShapeDtypeStruct((TOP_K, t), F32)),
        grid=(t // tm,),
        in_specs=[pl.BlockSpec((tm, d), lambda i: (i, 0)),
                  pl.BlockSpec((d, n_exp), lambda i: (0, 0)),
                  pl.BlockSpec((n_exp, 1), lambda i: (0, 0))],
        out_specs=(pl.BlockSpec((TOP_K, tm), lambda i: (0, i)),
                   pl.BlockSpec((TOP_K, tm), lambda i: (0, i))),
        compiler_params=_params("parallel"),
        name="router",
    )(hb, w_router, bias)


def _expert_kernel(b0_ref, nb_ref, nu_ref, x_hbm, wg_ref, wu_ref, wd_ref, y_hbm, xbuf, ybuf, sem_in, sem_out,
                   w1_scr, wd_scr, *, f, up_off, n_blocks):
    e = pl.program_id(0)
    b0 = b0_ref[e]
    nb = nb_ref[e]

    def x_copy(blk, slot):
        return pltpu.make_async_copy(x_hbm.at[pl.ds(blk * MOE_ROWS, MOE_ROWS)], xbuf.at[slot], sem_in.at[slot])

    def y_copy(blk, slot):
        return pltpu.make_async_copy(ybuf.at[slot], y_hbm.at[pl.ds(blk * MOE_ROWS, MOE_ROWS)], sem_out.at[slot])

    @pl.when(e == 0)
    def _():
        w1_scr[f:up_off, :] = jnp.zeros((up_off - f, w1_scr.shape[1]), BF16)

    @pl.when(nb > 0)
    def _():
        x_copy(b0, 0).start(priority=1)
        w1_scr[0:f, :] = wg_ref[0].astype(BF16)
        w1_scr[up_off:up_off + f, :] = wu_ref[0].astype(BF16)
        wd_scr[...] = wd_ref[0].astype(BF16)

        def chunk(c, carry):
            slot = c % 2
            x_copy(b0 + c, slot).wait()

            @pl.when(c + 1 < nb)
            def _():
                x_copy(b0 + c + 1, 1 - slot).start(priority=1)

            @pl.when(c >= 2)
            def _():
                y_copy(b0 + c - 2, slot).wait()

            h = _dot_t(xbuf[slot], w1_scr[...])
            a = jax.nn.silu(h[:, :f]) * h[:, up_off:up_off + f]
            ybuf[slot] = _dot(a.astype(BF16), wd_scr[...]).astype(ybuf.dtype)
            y_copy(b0 + c, slot).start(priority=1)
            return carry

        lax.fori_loop(0, nb, chunk, 0)

        @pl.when(nb >= 2)
        def _():
            y_copy(b0 + nb - 2, nb % 2).wait()

        y_copy(b0 + nb - 1, (nb - 1) % 2).wait()

    @pl.when(e == pl.num_programs(0) - 1)
    def _():
        ybuf[0] = jnp.zeros(ybuf.shape[1:], ybuf.dtype)

        def fill(blk, carry):
            y_copy(blk, 0).start()
            y_copy(blk, 0).wait()
            return carry

        lax.fori_loop(nu_ref[0], n_blocks, fill, 0)


def _experts(blk0, nblk, n_used, x_sorted, w_gate_t, w_up_t, w_down):
    n_rows, d = x_sorted.shape
    n_exp, f, _ = w_down.shape
    up_off = -(-f // LANES) * LANES
    n_blocks = n_rows // MOE_ROWS
    wspec = pl.BlockSpec((1, f, d), lambda e, b0, nb, nu: (e, 0, 0))
    grid_spec = pltpu.PrefetchScalarGridSpec(
        num_scalar_prefetch=3,
        grid=(n_exp,),
        in_specs=[pl.BlockSpec(memory_space=pl.ANY), wspec, wspec, wspec],
        out_specs=pl.BlockSpec(memory_space=pl.ANY),
        scratch_shapes=[pltpu.VMEM((2, MOE_ROWS, d), BF16), pltpu.VMEM((2, MOE_ROWS, d), BF16),
                        pltpu.SemaphoreType.DMA((2,)), pltpu.SemaphoreType.DMA((2,)),
                        pltpu.VMEM((up_off + f, d), BF16), pltpu.VMEM((f, d), BF16)],
    )
    return pl.pallas_call(
        functools.partial(_expert_kernel, f=f, up_off=up_off, n_blocks=n_blocks),
        out_shape=jax.ShapeDtypeStruct((n_rows, d), BF16),
        grid_spec=grid_spec,
        compiler_params=_params("arbitrary"),
        name="experts",
    )(blk0, nblk, n_used, x_sorted, w_gate_t, w_up_t, w_down)


def _final_kernel(h_ref, hb_ref, yg_ref, wt_ref, wg_ref, wu_ref, wd_ref, g_ref, b_ref, o_ref, *, alpha):
    hb = hb_ref[...]
    a = jax.nn.silu(_dot_t(hb, wg_ref[...])) * _dot_t(hb, wu_ref[...])
    shared = _dot(a.astype(BF16), wd_ref[...])
    wt = wt_ref[...]
    routed = wt[:, 0:1] * yg_ref[0].astype(F32)
    for k in range(1, yg_ref.shape[0]):
        routed = routed + wt[:, k:k + 1] * yg_ref[k].astype(F32)
    y = alpha * h_ref[...] + (routed + shared)
    o_ref[...] = _layer_norm(y, g_ref[...], b_ref[...])


def _final(h, hb, yg, wt, ws_gate_t, ws_up_t, ws_down, g, b, alpha, *, tm=256):
    t, d = h.shape
    f = ws_down.shape[0]
    tm = _tile(t, tm)
    row = pl.BlockSpec((tm, d), lambda i: (i, 0))
    vec = pl.BlockSpec((1, d), lambda i: (0, 0))
    wfull = pl.BlockSpec((f, d), lambda i: (0, 0))
    return pl.pallas_call(
        functools.partial(_final_kernel, alpha=alpha),
        out_shape=jax.ShapeDtypeStruct((t, d), F32),
        grid=(t // tm,),
        in_specs=[row, row, pl.BlockSpec((yg.shape[0], tm, d), lambda i: (0, i, 0)),
                  pl.BlockSpec((tm, wt.shape[1]), lambda i: (i, 0)), wfull, wfull, wfull, vec, vec],
        out_specs=row,
        compiler_params=_params("parallel"),
        name="shared_combine_ln2",
    )(h, hb, yg, wt, ws_gate_t, ws_up_t, ws_down, g, b)


def _dispatch(idx_t, n_exp):
    top_k, t = idx_t.shape
    tk = t * top_k
    onehot = (idx_t[:, :, None] == jnp.arange(n_exp, dtype=jnp.int32)).astype(jnp.int32)
    per_tok = jnp.sum(onehot, axis=0)
    cum = jnp.cumsum(per_tok, axis=0)
    counts = cum[-1]
    pcounts = (counts + MOE_ROWS - 1) // MOE_ROWS * MOE_ROWS
    pend = jnp.cumsum(pcounts)
    pstart = pend - pcounts
    dest = jnp.sum(onehot * (cum - per_tok + pstart)[None], axis=-1)
    n_blocks = -(-tk // MOE_ROWS) + n_exp
    n_used = (pend[-1:] // MOE_ROWS).astype(jnp.int32)
    n_rows = n_blocks * MOE_ROWS
    assert n_rows == tk + n_exp * MOE_ROWS
    i = jnp.arange(MOE_ROWS, dtype=jnp.int32)[None, :]
    pad_keys = jnp.where(i < (pcounts - counts)[:, None], (pstart + counts)[:, None] + i, n_rows)
    keys = jnp.concatenate([dest.reshape(-1), pad_keys.reshape(-1)]).astype(jnp.int32)
    toks = jnp.concatenate([jnp.tile(jnp.arange(t, dtype=jnp.int32), top_k),
                            jnp.arange(n_exp * MOE_ROWS, dtype=jnp.int32) % t])
    _, row_tok = lax.sort((keys, toks), num_keys=1)
    return dest, row_tok, (pstart // MOE_ROWS).astype(jnp.int32), (pcounts // MOE_ROWS).astype(jnp.int32), n_used


def _moe(h, hb, w_router, router_bias, w_gate_t, w_up_t, w_down, ws_gate_t, ws_up_t, ws_down, g, b, alpha):
    t, d = h.shape
    n_exp = w_router.shape[1]
    idx_t, wt_t = _router(hb, w_router, router_bias.reshape(n_exp, 1))
    dest, row_tok, blk0, nblk, n_used = _dispatch(idx_t, n_exp)
    x_sorted = hb[row_tok]
    y_sorted = _experts(blk0, nblk, n_used, x_sorted, w_gate_t, w_up_t, w_down)
    yg = y_sorted[dest]
    return _final(h, hb, yg, wt_t.T, ws_gate_t, ws_up_t, ws_down, g, b, alpha)


def _rope_tables(pos):
    half = B_HEAD_DIM // 2
    inv = 1.0 / (ROPE_THETA ** (jnp.arange(half, dtype=F32) / half))
    ang = pos.astype(F32)[:, None] * inv[None, :]
    cos, sin = jnp.cos(ang), jnp.sin(ang)
    reps = LANES // B_HEAD_DIM
    return (jnp.concatenate([cos, cos] * reps, axis=1), jnp.concatenate([-sin, sin] * reps, axis=1))


def kernel(x_prompt, x_sample, mem_prompt, cache_k, cache_v, page_table, cache_mem_k, cache_mem_v, w_in, a_ln_g, a_ln_b, a_ws, a_bs, lam_q1, lam_k1, lam_q2, lam_k2, b_subln_g, w_mk, w_mv, w_pa, w_pb, w_pc, w_o, ln1_g, ln1_b, w_router, router_bias, w_gate, w_up, w_down, ws_gate, ws_up, ws_down, ln2_g, ln2_b):
    depth = w_in.shape[0]
    assert depth == 1, "single-layer step"
    layer = 0
    batch, seq, d = x_prompt.shape
    n_seq, dec, _ = x_sample.shape
    n_mem = mem_prompt.shape[1]
    n_phys, page, n_bh = cache_k.shape[1:4]
    past = page_table.shape[1] * page
    c_width = cache_mem_k.shape[3] * C_HEAD_DIM
    a_width = A_GROUPS * A_GROUP_DIM
    qk_width = n_bh * 2 * B_HEAD_DIM
    bv_width = n_bh * B_V_DIM
    assert CHUNK % dec == 0 and seq % CHUNK == 0
    alpha = (2.0 * depth) ** 0.25
    lam_init = _lambda_init(layer)
    tp, ts = batch * seq, n_seq * dec
    t = tp + ts

    x = jnp.concatenate([x_prompt.reshape(tp, d), x_sample.reshape(ts, d)], axis=0)
    xb = x.astype(BF16)
    w_in_b = w_in[layer].astype(BF16)
    o = [0, 2 * a_width]
    for wdt in (qk_width, qk_width, bv_width, c_width, 3 * d):
        o.append(o[-1] + wdt)
    pos = jnp.concatenate([jnp.tile(jnp.arange(seq), batch), past + jnp.tile(jnp.arange(dec), n_seq)])
    rope = _rope_tables(pos)

    reps = CHUNK // dec
    ws = a_ws[layer]
    ws_s = jnp.einsum("ab,gts->gatbs", jnp.eye(reps, dtype=ws.dtype), ws[:, :dec, :dec]).reshape(
        A_GROUPS, CHUNK, CHUNK)
    ws2 = jnp.stack([ws, ws_s]).astype(BF16)
    bs = a_bs[layer]
    bs2 = jnp.stack([bs, jnp.tile(bs[:, :dec], (1, reps))])
    bs2 = jnp.broadcast_to(bs2[..., None], bs2.shape + (CHUNK,)).astype(F32)

    o_a, vn = _proj_a(xb, w_in_b[:, o[0]:o[1]], a_ln_g[layer].reshape(1, a_width),
                      a_ln_b[layer].reshape(1, a_width), ws2, bs2, tp)
    q = _mm(xb, w_in_b[:, o[1]:o[2]], BF16, epilogue="rope", scale=B_HEAD_DIM ** -0.5, rope=rope)
    v = _mm(xb, w_in_b[:, o[3]:o[4]], F32)
    cq = _mm(xb, w_in_b[:, o[4]:o[5]], BF16)
    gates = _mm(xb, w_in_b[:, o[5]:o[6]], BF16, epilogue="sigmoid")
    w_k = w_in_b[:, o[2]:o[3]]
    w_k_t = w_k.T
    cos_t, sin_t = rope[0][:, :B_HEAD_DIM // 2].T, rope[1][:, B_HEAD_DIM // 2:B_HEAD_DIM].T
    kt_p = _proj_kt(w_k_t, xb, cos_t, sin_t, batch, tp, tm=512)
    xs_t = jnp.swapaxes(x_sample, 0, 1).reshape(ts, d).astype(BF16)
    pos_s = past + jnp.repeat(jnp.arange(dec), n_seq)
    cos_s, sin_s = _rope_tables(pos_s)
    kt_s = _proj_kt(w_k_t, xs_t, cos_s[:, :B_HEAD_DIM // 2].T, sin_s[:, B_HEAD_DIM // 2:B_HEAD_DIM].T, dec,
                    ts, tm=n_seq)
    k_s = _mm(xb[tp:], w_k, F32, epilogue="rope", rope=(rope[0][tp:], rope[1][tp:]))

    lamv = jnp.stack([lam_q1[layer], lam_k1[layer], lam_q2[layer], lam_k2[layer]]).astype(F32)
    g_sub = b_subln_g[layer].reshape(1, B_V_DIM)
    ob_p = _attn_prompt(lamv, q, kt_p, v, g_sub, batch, seq, n_bh, lam_init)
    cache_kt = jnp.transpose(cache_k[layer], (0, 2, 3, 4, 1)).reshape(n_phys, n_bh, 2 * B_HEAD_DIM, page)
    ob_s = _attn_sample(page_table, lamv, q[tp:].reshape(n_seq, dec, qk_width),
                        k_s.reshape(n_seq, dec, qk_width), v[tp:].reshape(n_seq, dec, bv_width), g_sub,
                        cache_kt, cache_v[layer].reshape(n_phys, page * n_bh, B_V_DIM), lam_init)
    o_b = jnp.concatenate([ob_p, ob_s.reshape(ts, bv_width)], axis=0)

    c_heads = c_width // C_HEAD_DIM
    memb = mem_prompt.reshape(batch * n_mem, d).astype(BF16)
    mk = _mm(memb, w_mk[layer].astype(BF16), F32)
    mv = _mm(memb, w_mv[layer].astype(BF16), F32)
    oc_p = _mem_attn(cq[:tp].reshape(batch, seq, c_width), mk.reshape(batch, n_mem, c_width),
                     mv.reshape(batch, n_mem, c_width), interleaved=False)
    oc_s = _mem_attn(cq[tp:].reshape(n_seq, dec, c_width),
                     cache_mem_k[layer].reshape(n_seq, n_mem * c_heads, C_HEAD_DIM),
                     cache_mem_v[layer].reshape(n_seq, n_mem * c_heads, C_HEAD_DIM), interleaved=True)
    o_c = jnp.concatenate([oc_p.reshape(tp, c_width), oc_s.reshape(ts, c_width)], axis=0)

    m = _merge(o_a, o_b, o_c, gates, w_pa[layer].astype(BF16), w_pb[layer].astype(BF16),
               w_pc[layer].astype(BF16))
    h1, h1b = _out_proj(m, w_o[layer].astype(BF16), x, ln1_g[layer].reshape(1, d), ln1_b[layer].reshape(1, d),
                        alpha)
    y = _moe(h1, h1b, w_router[layer].astype(BF16), router_bias[layer],
             jnp.swapaxes(w_gate[layer], 1, 2), jnp.swapaxes(w_up[layer], 1, 2), w_down[layer],
             ws_gate[layer].T.astype(BF16), ws_up[layer].T.astype(BF16), ws_down[layer].astype(BF16),
             ln2_g[layer].reshape(1, d), ln2_b[layer].reshape(1, d), alpha)

    k_prompt = jnp.transpose(kt_p.reshape(batch, n_bh, 2, B_HEAD_DIM, seq), (0, 4, 1, 2, 3))
    k_sample = jnp.transpose(kt_s.reshape(dec, n_bh, 2, B_HEAD_DIM, n_seq), (4, 0, 1, 2, 3))
    return (y[:tp].reshape(batch, seq, d),
            y[tp:].reshape(n_seq, dec, d),
            k_prompt[None],
            v[:tp].reshape(1, batch, seq, n_bh, B_V_DIM),
            mk.reshape(1, batch, n_mem, c_heads, C_HEAD_DIM),
            mv.reshape(1, batch, n_mem, c_heads, C_HEAD_DIM),
            k_sample[None],
            v[tp:].reshape(1, n_seq, dec, n_bh, B_V_DIM),
            vn[tp:].reshape(1, n_seq, dec, a_width))
```

```python
import functools
import math

import jax
import jax.numpy as jnp
from jax import lax
from jax.experimental import pallas as pl
from jax.experimental.pallas import tpu as pltpu

F32 = jnp.float32
BF16 = jnp.bfloat16

LN_EPS = 1e-5
ROPE_THETA = 10000.0
LANES = 128
A_GROUP_DIM = 128
A_GROUPS = 4
CHUNK = 128
B_HEAD_DIM = 64
B_V_DIM = 128
C_HEAD_DIM = 128
N_GROUPS = 8
TOPK_GROUPS = 4
TOP_K = 8
ROUTED_SCALE = 2.5
MOE_ROWS = 256
VMEM_LIMIT = 56 * 1024 * 1024


def _lambda_init(layer):
    return 0.8 - 0.6 * math.exp(-0.3 * layer)


def _params(*sem):
    return pltpu.CompilerParams(dimension_semantics=sem, vmem_limit_bytes=VMEM_LIMIT)


def _tile(n, pref):
    t = min(n, pref)
    while n % t:
        t //= 2
    return t


def _dot(a, b):
    return jnp.dot(a, b, preferred_element_type=F32)


def _dot_t(a, b):
    return lax.dot_general(a, b, (((1,), (1,)), ((), ())), preferred_element_type=F32)


def _mm_kernel(*refs, epilogue, scale):
    if epilogue == "rope":
        x_ref, w_ref, cos_ref, sin_ref, o_ref = refs
    else:
        x_ref, w_ref, o_ref = refs
    acc = _dot(x_ref[...], w_ref[...])
    if epilogue == "rope":
        n = acc.shape[1] // LANES
        cos = jnp.concatenate([cos_ref[...]] * n, axis=1)
        sin = jnp.concatenate([sin_ref[...]] * n, axis=1)
        lane = lax.broadcasted_iota(jnp.int32, acc.shape, 1)
        first = (lane % B_HEAD_DIM) < (B_HEAD_DIM // 2)
        half = B_HEAD_DIM // 2
        swapped = jnp.where(first, pltpu.roll(acc, acc.shape[1] - half, 1), pltpu.roll(acc, half, 1))
        acc = acc * cos + swapped * sin
    elif epilogue == "sigmoid":
        acc = jax.nn.sigmoid(acc)
    if scale != 1.0:
        acc = acc * scale
    o_ref[...] = acc.astype(o_ref.dtype)


def _mm(x, w, out_dtype, *, epilogue="none", scale=1.0, rope=None, tn=1024):
    m, k = x.shape
    n = w.shape[1]
    tm, tn = _tile(m, 512 if epilogue == "rope" else 1024), _tile(n, tn)
    in_specs = [pl.BlockSpec((tm, k), lambda i, j: (i, 0)),
                pl.BlockSpec((k, tn), lambda i, j: (0, j))]
    args = [x, w]
    if epilogue == "rope":
        in_specs += [pl.BlockSpec((tm, LANES), lambda i, j: (i, 0))] * 2
        args += list(rope)
    return pl.pallas_call(
        functools.partial(_mm_kernel, epilogue=epilogue, scale=scale),
        out_shape=jax.ShapeDtypeStruct((m, n), out_dtype),
        grid=(m // tm, n // tn),
        in_specs=in_specs,
        out_specs=pl.BlockSpec((tm, tn), lambda i, j: (i, j)),
        compiler_params=_params("parallel", "arbitrary"),
        name="proj_" + epilogue,
    )(*args)


def _kt_kernel(w_ref, x_ref, cos_ref, sin_ref, o_ref):
    acc = _dot_t(w_ref[...], x_ref[...])
    cos, sin = cos_ref[...], sin_ref[...]
    half = B_HEAD_DIM // 2
    for g in range(acc.shape[0] // B_HEAD_DIM):
        lo = g * B_HEAD_DIM
        x1 = acc[lo:lo + half]
        x2 = acc[lo + half:lo + B_HEAD_DIM]
        o_ref[0, lo:lo + half, :] = x1 * cos - x2 * sin
        o_ref[0, lo + half:lo + B_HEAD_DIM, :] = x2 * cos + x1 * sin


def _proj_kt(w_t, x, cos_t, sin_t, n_groups, m, *, tm):
    n, k = w_t.shape
    per = m // n_groups
    tm = _tile(per, tm)
    nt = per // tm
    half = B_HEAD_DIM // 2
    return pl.pallas_call(
        _kt_kernel,
        out_shape=jax.ShapeDtypeStruct((n_groups, n, per), F32),
        grid=(m // tm,),
        in_specs=[pl.BlockSpec((n, k), lambda i: (0, 0)),
                  pl.BlockSpec((tm, k), lambda i: (i, 0)),
                  pl.BlockSpec((half, tm), lambda i: (0, i)),
                  pl.BlockSpec((half, tm), lambda i: (0, i))],
        out_specs=pl.BlockSpec((1, n, tm), lambda i: (i // nt, 0, i % nt)),
        compiler_params=_params("parallel"),
        name="proj_kt",
    )(w_t, x, cos_t, sin_t)


def _proj_a_kernel(x_ref, w_ref, lng_ref, lnb_ref, ws_ref, bs_ref, oa_ref, vn_ref):
    h = _dot(x_ref[...], w_ref[...])
    aw = h.shape[1] // 2
    u = jax.nn.gelu(h[:, :aw])
    g = jax.nn.gelu(h[:, aw:])
    mu = jnp.mean(g, axis=-1, keepdims=True)
    var = jnp.mean(jnp.square(g - mu), axis=-1, keepdims=True)
    vn = (g - mu) * lax.rsqrt(var + LN_EPS) * lng_ref[...] + lnb_ref[...]
    vn_ref[...] = vn
    vnb = vn.astype(BF16)
    row = lax.broadcasted_iota(jnp.int32, (CHUNK, CHUNK), 0)
    col = lax.broadcasted_iota(jnp.int32, (CHUNK, CHUNK), 1)
    causal = col <= row
    for gi in range(A_GROUPS):
        wsg = jnp.where(causal, ws_ref[0, gi], jnp.zeros((), BF16))
        cols = slice(gi * A_GROUP_DIM, (gi + 1) * A_GROUP_DIM)
        for c in range(h.shape[0] // CHUNK):
            rows = slice(c * CHUNK, (c + 1) * CHUNK)
            s = _dot(wsg, vnb[rows, cols]) + bs_ref[0, gi]
            oa_ref[rows, cols] = (u[rows, cols] * s).astype(oa_ref.dtype)


def _proj_a(x, w, lng, lnb, ws2, bs2, n_prompt, *, tm=512):
    m, k = x.shape
    n = w.shape[1]
    tm = _tile(math.gcd(m, n_prompt), tm)
    npt = n_prompt // tm
    sel = lambda i: (jnp.where(i >= npt, 1, 0), 0, 0, 0)
    return pl.pallas_call(
        _proj_a_kernel,
        out_shape=(jax.ShapeDtypeStruct((m, n // 2), BF16), jax.ShapeDtypeStruct((m, n // 2), F32)),
        grid=(m // tm,),
        in_specs=[pl.BlockSpec((tm, k), lambda i: (i, 0)),
                  pl.BlockSpec((k, n), lambda i: (0, 0)),
                  pl.BlockSpec((1, n // 2), lambda i: (0, 0)),
                  pl.BlockSpec((1, n // 2), lambda i: (0, 0)),
                  pl.BlockSpec((1, A_GROUPS, CHUNK, CHUNK), sel),
                  pl.BlockSpec((1, A_GROUPS, CHUNK, CHUNK), sel)],
        out_specs=(pl.BlockSpec((tm, n // 2), lambda i: (i, 0)),
                   pl.BlockSpec((tm, n // 2), lambda i: (i, 0))),
        compiler_params=_params("parallel"),
        name="proj_a",
    )(x, w, lng, lnb, ws2, bs2)


def _diff_lambda(lam_ref, lam_init):
    lv = lam_ref[...]
    s1 = jnp.sum(lv[0:1] * lv[1:2], axis=-1, keepdims=True)
    s2 = jnp.sum(lv[2:3] * lv[3:4], axis=-1, keepdims=True)
    return jnp.exp(s1) - jnp.exp(s2) + lam_init


def _stack_maps(q):
    lane = lax.broadcasted_iota(jnp.int32, q.shape, 1)
    zero = jnp.zeros((), q.dtype)
    return jnp.concatenate([jnp.where(lane < B_HEAD_DIM, q, zero),
                            jnp.where(lane >= B_HEAD_DIM, q, zero)], axis=0)


def _sub_norm(o, g, lam_init):
    r = lax.rsqrt(jnp.mean(o * o, axis=-1, keepdims=True) + LN_EPS)
    return o * r * g * (1.0 - lam_init)


def _attn_p_kernel(lam_ref, q_ref, kt_ref, v_ref, g_ref, o_ref, kt_scr, vb_scr, *, tq, nh, lam_init):
    qi = pl.program_id(2)
    hd = B_V_DIM

    @pl.when(qi == 0)
    def _():
        for kb in range(kt_scr.shape[0]):
            kt_scr[kb] = kt_ref[0, :, kb * tq:(kb + 1) * tq].astype(BF16)
        vb_scr[...] = v_ref[...].astype(BF16)

    qs = [_stack_maps(q_ref[:, h * hd:(h + 1) * hd]) for h in range(nh)]

    def step(carry, kb, diag):
        off = pl.multiple_of(kb * tq, tq)
        kt = kt_scr[kb]
        out = []
        for h in range(nh):
            s = _dot(qs[h], kt[h * hd:(h + 1) * hd])
            if diag:
                row = lax.broadcasted_iota(jnp.int32, s.shape, 0) % tq
                col = lax.broadcasted_iota(jnp.int32, s.shape, 1)
                s = jnp.where(col <= row, s, -jnp.inf)
            m, l, acc = carry[h]
            m_new = jnp.maximum(m, jnp.max(s, axis=-1, keepdims=True))
            a = jnp.exp(m - m_new)
            p = jnp.exp(s - m_new)
            l = a * l + jnp.sum(p, axis=-1, keepdims=True)
            acc = a * acc + _dot(p.astype(BF16), vb_scr[pl.ds(off, tq), h * hd:(h + 1) * hd])
            out.append((m_new, l, acc))
        return tuple(out)

    init = tuple((jnp.full((2 * tq, 1), -jnp.inf, F32), jnp.zeros((2 * tq, 1), F32),
                  jnp.zeros((2 * tq, hd), F32)) for _ in range(nh))
    carry = lax.fori_loop(0, qi, lambda kb, c: step(c, kb, False), init)
    carry = step(carry, qi, True)
    lam = _diff_lambda(lam_ref, lam_init)
    for h in range(nh):
        _, l, acc = carry[h]
        o = acc / l
        o = o[:tq] - lam * o[tq:]
        o_ref[:, h * hd:(h + 1) * hd] = _sub_norm(o, g_ref[...], lam_init).astype(o_ref.dtype)


def _attn_prompt(lamv, q, kt, v, g, batch, seq, n_heads, lam_init, *, tq=256, nh=2):
    tq = _tile(seq, tq)
    nq = seq // tq
    nh = math.gcd(nh, n_heads)
    w = nh * B_V_DIM
    return pl.pallas_call(
        functools.partial(_attn_p_kernel, tq=tq, nh=nh, lam_init=lam_init),
        out_shape=jax.ShapeDtypeStruct((batch * seq, n_heads * B_V_DIM), BF16),
        grid=(batch, n_heads // nh, nq),
        in_specs=[pl.BlockSpec(lamv.shape, lambda b, h, i: (0, 0)),
                  pl.BlockSpec((tq, w), lambda b, h, i: (b * nq + i, h)),
                  pl.BlockSpec((1, w, seq), lambda b, h, i: (b, h, 0)),
                  pl.BlockSpec((seq, w), lambda b, h, i: (b, h)),
                  pl.BlockSpec((1, B_V_DIM), lambda b, h, i: (0, 0))],
        out_specs=pl.BlockSpec((tq, w), lambda b, h, i: (b * nq + i, h)),
        scratch_shapes=[pltpu.VMEM((nq, w, tq), BF16), pltpu.VMEM((seq, w), BF16)],
        compiler_params=_params("parallel", "parallel", "arbitrary"),
        name="attn_prompt",
    )(lamv, q, kt, v, g)


def _attn_s_kernel(pt_ref, lam_ref, q_ref, kn_ref, vn_ref, g_ref, *refs, n_pages, n_heads, page, dec,
                   lam_init):
    kp = refs[:n_pages]
    vp = refs[n_pages:2 * n_pages]
    o_ref = refs[2 * n_pages]
    s_scr, kn_scr, vn_scr = refs[2 * n_pages + 1:]

    @pl.when(pl.program_id(0) == 0)
    def _():
        kn_scr[...] = jnp.zeros_like(kn_scr)
        vn_scr[...] = jnp.zeros_like(vn_scr)

    kn_scr[0:dec, :] = kn_ref[0]
    vn_scr[0:dec, :] = vn_ref[0]
    lam = _diff_lambda(lam_ref, lam_init)
    past = n_pages * page
    for h in range(n_heads):
        cols = slice(h * B_V_DIM, (h + 1) * B_V_DIM)
        qs = _stack_maps(q_ref[0, :, cols])
        for p in range(n_pages):
            s_scr[:, p * page:(p + 1) * page] = _dot(qs, kp[p][0, h].astype(BF16))
        sn = _dot_t(qs, kn_scr[:, cols].astype(BF16))
        row = lax.broadcasted_iota(jnp.int32, sn.shape, 0) % dec
        col = lax.broadcasted_iota(jnp.int32, sn.shape, 1)
        s_scr[:, past:] = jnp.where(col <= row, sn, -jnp.inf)
        s = s_scr[...]
        m = jnp.max(s, axis=-1, keepdims=True)
        e = jnp.exp(s - m)
        pr = (e / jnp.sum(e, axis=-1, keepdims=True)).astype(BF16)
        o = _dot(pr[:, past:], vn_scr[:, cols].astype(BF16))
        for p in range(n_pages):
            vh = vp[p][0, pl.ds(h, page, stride=n_heads), :]
            o = o + _dot(pr[:, p * page:(p + 1) * page], vh.astype(BF16))
        o = o[:dec] - lam * o[dec:]
        o_ref[0, :, cols] = _sub_norm(o, g_ref[...], lam_init).astype(o_ref.dtype)


def _attn_sample(page_table, lamv, q, k_new, v_new, g, cache_kt, cache_v, lam_init):
    n_seq, dec, width = q.shape
    n_pages = page_table.shape[1]
    n_heads = width // B_V_DIM
    page = cache_kt.shape[3]
    tok = lambda b, pt: (b, 0, 0)
    const = lambda b, pt: (0, 0)

    def k_spec(p):
        return pl.BlockSpec((1, n_heads, B_V_DIM, page), lambda b, pt: (pt[b, p], 0, 0, 0))

    def v_spec(p):
        return pl.BlockSpec((1, page * n_heads, B_V_DIM), lambda b, pt: (pt[b, p], 0, 0))

    grid_spec = pltpu.PrefetchScalarGridSpec(
        num_scalar_prefetch=1,
        grid=(n_seq,),
        in_specs=[pl.BlockSpec(lamv.shape, const),
                  pl.BlockSpec((1, dec, width), tok),
                  pl.BlockSpec((1, dec, width), tok),
                  pl.BlockSpec((1, dec, width), tok),
                  pl.BlockSpec((1, B_V_DIM), const)]
                 + [k_spec(p) for p in range(n_pages)]
                 + [v_spec(p) for p in range(n_pages)],
        out_specs=pl.BlockSpec((1, dec, width), tok),
        scratch_shapes=[pltpu.VMEM((2 * dec, (n_pages + 1) * page), F32),
                        pltpu.VMEM((page, width), F32),
                        pltpu.VMEM((page, width), F32)],
    )
    return pl.pallas_call(
        functools.partial(_attn_s_kernel, n_pages=n_pages, n_heads=n_heads, page=page, dec=dec,
                          lam_init=lam_init),
        out_shape=jax.ShapeDtypeStruct((n_seq, dec, width), BF16),
        grid_spec=grid_spec,
        compiler_params=_params("arbitrary"),
        name="attn_sample",
    )(page_table, lamv, q, k_new, v_new, g, *([cache_kt] * n_pages), *([cache_v] * n_pages))


def _mem_attn_kernel(q_ref, k_ref, v_ref, o_ref, *, n_heads, interleaved):
    tq = q_ref.shape[1]
    pad = (-tq) % 16
    n_mem = k_ref.shape[1] // n_heads if interleaved else k_ref.shape[1]
    for h in range(n_heads):
        cols = slice(h * C_HEAD_DIM, (h + 1) * C_HEAD_DIM)
        if interleaved:
            kh = k_ref[0, pl.ds(h, n_mem, stride=n_heads), :]
            vh = v_ref[0, pl.ds(h, n_mem, stride=n_heads), :]
        else:
            kh, vh = k_ref[0, :, cols], v_ref[0, :, cols]
        q = q_ref[0, :, cols]
        if pad:
            q = jnp.concatenate([q, jnp.zeros((pad, C_HEAD_DIM), q.dtype)], axis=0)
        s = _dot_t(q, kh.astype(BF16)) * (C_HEAD_DIM ** -0.5)
        m = jnp.max(s, axis=-1, keepdims=True)
        e = jnp.exp(s - m)
        pr = (e / jnp.sum(e, axis=-1, keepdims=True)).astype(BF16)
        o = _dot(pr, vh.astype(BF16))
        o_ref[0, :, cols] = o[:tq].astype(o_ref.dtype)


def _mem_attn(q, mk, mv, *, interleaved, tq=512):
    b, t, w = q.shape
    tq = _tile(t, tq)
    kv_spec = pl.BlockSpec((1,) + mk.shape[1:], lambda i, j: (i, 0, 0))
    return pl.pallas_call(
        functools.partial(_mem_attn_kernel, n_heads=w // C_HEAD_DIM, interleaved=interleaved),
        out_shape=jax.ShapeDtypeStruct((b, t, w), BF16),
        grid=(b, t // tq),
        in_specs=[pl.BlockSpec((1, tq, w), lambda i, j: (i, j, 0)), kv_spec, kv_spec],
        out_specs=pl.BlockSpec((1, tq, w), lambda i, j: (i, j, 0)),
        compiler_params=_params("parallel", "arbitrary"),
        name="mem_attn",
    )(q, mk, mv)


def _merge_kernel(oa_ref, ob_ref, oc_ref, ga_ref, gb_ref, gc_ref, wa_ref, wb_ref, wc_ref, m_ref):
    m = ga_ref[...].astype(F32) * _dot(oa_ref[...], wa_ref[...])
    m = m + gb_ref[...].astype(F32) * _dot(ob_ref[...], wb_ref[...])
    m = m + gc_ref[...].astype(F32) * _dot(oc_ref[...], wc_ref[...])
    m_ref[...] = m.astype(m_ref.dtype)


def _merge(oa, ob, oc, gates, wa, wb, wc, *, tm=1024, tn=512):
    t = oa.shape[0]
    d = wa.shape[1]
    tm, tn = _tile(t, tm), _tile(d, tn)
    nj = d // tn
    row = lambda width: pl.BlockSpec((tm, width), lambda i, j: (i, 0))
    gate = lambda b: pl.BlockSpec((tm, tn), lambda i, j: (i, b * nj + j))
    wcol = lambda width: pl.BlockSpec((width, tn), lambda i, j: (0, j))
    return pl.pallas_call(
        _merge_kernel,
        out_shape=jax.ShapeDtypeStruct((t, d), BF16),
        grid=(t // tm, nj),
        in_specs=[row(oa.shape[1]), row(ob.shape[1]), row(oc.shape[1]), gate(0), gate(1), gate(2),
                  wcol(wa.shape[0]), wcol(wb.shape[0]), wcol(wc.shape[0])],
        out_specs=pl.BlockSpec((tm, tn), lambda i, j: (i, j)),
        compiler_params=_params("parallel", "arbitrary"),
        name="merge",
    )(oa, ob, oc, gates, gates, gates, wa, wb, wc)


def _layer_norm(x, g, b):
    mu = jnp.mean(x, axis=-1, keepdims=True)
    var = jnp.mean(jnp.square(x - mu), axis=-1, keepdims=True)
    return (x - mu) * lax.rsqrt(var + LN_EPS) * g + b


def _out_kernel(m_ref, w_ref, x_ref, g_ref, b_ref, h_ref, hb_ref, *, alpha):
    y = alpha * x_ref[...] + _dot(m_ref[...], w_ref[...])
    h = _layer_norm(y, g_ref[...], b_ref[...])
    h_ref[...] = h
    hb_ref[...] = h.astype(BF16)


def _out_proj(m, w_o, x, g, b, alpha, *, tm=512):
    t, d = x.shape
    tm = _tile(t, tm)
    row = pl.BlockSpec((tm, d), lambda i: (i, 0))
    vec = pl.BlockSpec((1, d), lambda i: (0, 0))
    return pl.pallas_call(
        functools.partial(_out_kernel, alpha=alpha),
        out_shape=(jax.ShapeDtypeStruct((t, d), F32), jax.ShapeDtypeStruct((t, d), BF16)),
        grid=(t // tm,),
        in_specs=[row, pl.BlockSpec((d, d), lambda i: (0, 0)), row, vec, vec],
        out_specs=(row, row),
        compiler_params=_params("parallel"),
        name="out_proj_ln1",
    )(m, w_o, x, g, b)


def _first_argmax(vals, idx, sentinel):
    m = jnp.max(vals, axis=0, keepdims=True)
    first = jnp.min(jnp.where(vals == m, idx, sentinel), axis=0, keepdims=True)
    return m, first


def _router_kernel(x_ref, w_ref, bias_ref, idx_ref, wt_ref):
    s = jax.nn.sigmoid(_dot(x_ref[...], w_ref[...]))
    st = s.T
    sb = st + bias_ref[...]
    n_exp, tm = st.shape
    gs = n_exp // N_GROUPS
    neg = -jnp.inf
    ridx = lax.broadcasted_iota(jnp.int32, (gs, tm), 0)
    grp = []
    for g in range(N_GROUPS):
        blk = sb[g * gs:(g + 1) * gs]
        m1, first = _first_argmax(blk, ridx, gs)
        m2 = jnp.max(jnp.where(ridx == first, neg, blk), axis=0, keepdims=True)
        grp.append(m1 + m2)
    work = jnp.concatenate(grp, axis=0)
    gidx = lax.broadcasted_iota(jnp.int32, work.shape, 0)
    gmask = jnp.zeros(work.shape, jnp.bool_)
    for _ in range(TOPK_GROUPS):
        _, first = _first_argmax(work, gidx, N_GROUPS)
        sel = gidx == first
        gmask = jnp.logical_or(gmask, sel)
        work = jnp.where(sel, neg, work)
    masked = jnp.concatenate(
        [jnp.where(gmask[g:g + 1], sb[g * gs:(g + 1) * gs], neg) for g in range(N_GROUPS)], axis=0)
    eidx = lax.broadcasted_iota(jnp.int32, masked.shape, 0)
    ids, wts = [], []
    for _ in range(TOP_K):
        _, first = _first_argmax(masked, eidx, n_exp)
        sel = eidx == first
        ids.append(first)
        wts.append(jnp.sum(jnp.where(sel, st, 0.0), axis=0, keepdims=True))
        masked = jnp.where(sel, neg, masked)
    wt = jnp.concatenate(wts, axis=0)
    idx_ref[...] = jnp.concatenate(ids, axis=0)
    wt_ref[...] = wt / jnp.sum(wt, axis=0, keepdims=True) * ROUTED_SCALE


def _router(hb, w_router, bias, *, tm=256):
    t, d = hb.shape
    n_exp = w_router.shape[1]
    tm = _tile(t, tm)
    return pl.pallas_call(
        _router_kernel,
        out_shape=(jax.ShapeDtypeStruct((TOP_K, t), jnp.int32), jax.ShapeDtypeStruct((TOP_K, t), F32)),
        grid=(t // tm,),
        in_specs=[pl.BlockSpec((tm, d), lambda i: (i, 0)),
                  pl.BlockSpec((d, n_exp), lambda i: (0, 0)),
                  pl.BlockSpec((n_exp, 1), lambda i: (0, 0))],
        out_specs=(pl.BlockSpec((TOP_K, tm), lambda i: (0, i)),
                   pl.BlockSpec((TOP_K, tm), lambda i: (0, i))),
        compiler_params=_params("parallel"),
        name="router",
    )(hb, w_router, bias)


def _expert_kernel(b0_ref, nb_ref, nu_ref, x_hbm, wg_hbm, wu_hbm, wd_hbm, y_hbm, xbuf, ybuf, wbuf, sem_in, sem_out,
                   sem_w, w1_scr, wd_scr, *, f, up_off, n_blocks):
    e = pl.program_id(0)
    last = pl.num_programs(0) - 1
    b0 = b0_ref[e]
    nb = nb_ref[e]
    n_used = nu_ref[0]

    def x_copy(blk):
        slot = blk % 2
        return pltpu.make_async_copy(x_hbm.at[pl.ds(blk * MOE_ROWS, MOE_ROWS)], xbuf.at[slot], sem_in.at[slot])

    def y_copy(blk):
        slot = blk % 2
        return pltpu.make_async_copy(ybuf.at[slot], y_hbm.at[pl.ds(blk * MOE_ROWS, MOE_ROWS)], sem_out.at[slot])

    def w_copies(ex):
        slot = ex % 2
        return [pltpu.make_async_copy(src.at[ex], wbuf.at[slot, j], sem_w.at[slot, j])
                for j, src in enumerate((wg_hbm, wu_hbm, wd_hbm))]

    @pl.when(e == 0)
    def _():
        for cp in w_copies(0):
            cp.start(priority=1)
        w1_scr[f:up_off, :] = jnp.zeros((up_off - f, w1_scr.shape[1]), BF16)

        @pl.when(n_used > 0)
        def _():
            x_copy(0).start()

    for cp in w_copies(e):
        cp.wait()

    @pl.when(e < last)
    def _():
        for cp in w_copies(e + 1):
            cp.start(priority=1)

    @pl.when(nb > 0)
    def _():
        wslot = e % 2
        w1_scr[0:f, :] = wbuf[wslot, 0].astype(BF16)
        w1_scr[up_off:up_off + f, :] = wbuf[wslot, 1].astype(BF16)
        wd_scr[...] = wbuf[wslot, 2].astype(BF16)

        def chunk(c, carry):
            g = b0 + c
            slot = g % 2
            x_copy(g).wait()

            @pl.when(g + 1 < n_used)
            def _():
                x_copy(g + 1).start()

            @pl.when(g >= 2)
            def _():
                y_copy(g - 2).wait()

            h = _dot_t(xbuf[slot], w1_scr[...])
            a = jax.nn.silu(h[:, :f]) * h[:, up_off:up_off + f]
            ybuf[slot] = _dot(a.astype(BF16), wd_scr[...]).astype(ybuf.dtype)
            y_copy(g).start()
            return carry

        lax.fori_loop(0, nb, chunk, 0)

    @pl.when(e == last)
    def _():
        @pl.when(n_used >= 2)
        def _():
            y_copy(n_used - 2).wait()

        @pl.when(n_used >= 1)
        def _():
            y_copy(n_used - 1).wait()

        ybuf[0] = jnp.zeros(ybuf.shape[1:], ybuf.dtype)

        def fill(blk, carry):
            cp = pltpu.make_async_copy(ybuf.at[0], y_hbm.at[pl.ds(blk * MOE_ROWS, MOE_ROWS)], sem_out.at[0])
            cp.start()
            cp.wait()
            return carry

        lax.fori_loop(n_used, n_blocks, fill, 0)


def _experts(blk0, nblk, n_used, x_sorted, w_gate_t, w_up_t, w_down):
    n_rows, d = x_sorted.shape
    n_exp, f, _ = w_down.shape
    up_off = -(-f // LANES) * LANES
    n_blocks = n_rows // MOE_ROWS
    hbm = pl.BlockSpec(memory_space=pl.ANY)
    grid_spec = pltpu.PrefetchScalarGridSpec(
        num_scalar_prefetch=3,
        grid=(n_exp,),
        in_specs=[hbm, hbm, hbm, hbm],
        out_specs=hbm,
        scratch_shapes=[pltpu.VMEM((2, MOE_ROWS, d), BF16), pltpu.VMEM((2, MOE_ROWS, d), BF16),
                        pltpu.VMEM((2, 3, f, d), F32),
                        pltpu.SemaphoreType.DMA((2,)), pltpu.SemaphoreType.DMA((2,)),
                        pltpu.SemaphoreType.DMA((2, 3)),
                        pltpu.VMEM((up_off + f, d), BF16), pltpu.VMEM((f, d), BF16)],
    )
    return pl.pallas_call(
        functools.partial(_expert_kernel, f=f, up_off=up_off, n_blocks=n_blocks),
        out_shape=jax.ShapeDtypeStruct((n_rows, d), BF16),
        grid_spec=grid_spec,
        compiler_params=_params("arbitrary"),
        name="experts",
    )(blk0, nblk, n_used, x_sorted, w_gate_t, w_up_t, w_down)


def _final_kernel(h_ref, hb_ref, yg_ref, wt_ref, wg_ref, wu_ref, wd_ref, g_ref, b_ref, o_ref, *, alpha):
    hb = hb_ref[...]
    a = jax.nn.silu(_dot_t(hb, wg_ref[...])) * _dot_t(hb, wu_ref[...])
    shared = _dot(a.astype(BF16), wd_ref[...])
    wt = wt_ref[...]
    routed = wt[:, 0:1] * yg_ref[0].astype(F32)
    for k in range(1, yg_ref.shape[0]):
        routed = routed + wt[:, k:k + 1] * yg_ref[k].astype(F32)
    y = alpha * h_ref[...] + (routed + shared)
    o_ref[...] = _layer_norm(y, g_ref[...], b_ref[...])


def _final(h, hb, yg, wt, ws_gate_t, ws_up_t, ws_down, g, b, alpha, *, tm=256):
    t, d = h.shape
    f = ws_down.shape[0]
    tm = _tile(t, tm)
    row = pl.BlockSpec((tm, d), lambda i: (i, 0))
    vec = pl.BlockSpec((1, d), lambda i: (0, 0))
    wfull = pl.BlockSpec((f, d), lambda i: (0, 0))
    return pl.pallas_call(
        functools.partial(_final_kernel, alpha=alpha),
        out_shape=jax.ShapeDtypeStruct((t, d), F32),
        grid=(t // tm,),
        in_specs=[row, row, pl.BlockSpec((yg.shape[0], tm, d), lambda i: (0, i, 0)),
                  pl.BlockSpec((tm, wt.shape[1]), lambda i: (i, 0)), wfull, wfull, wfull, vec, vec],
        out_specs=row,
        compiler_params=_params("parallel"),
        name="shared_combine_ln2",
    )(h, hb, yg, wt, ws_gate_t, ws_up_t, ws_down, g, b)


def _dispatch(idx_t, n_exp):
    top_k, t = idx_t.shape
    tk = t * top_k
    onehot = (idx_t[:, :, None] == jnp.arange(n_exp, dtype=jnp.int32)).astype(jnp.int32)
    per_tok = jnp.sum(onehot, axis=0)
    cum = jnp.cumsum(per_tok, axis=0)
    counts = cum[-1]
    pcounts = (counts + MOE_ROWS - 1) // MOE_ROWS * MOE_ROWS
    pend = jnp.cumsum(pcounts)
    pstart = pend - pcounts
    dest = jnp.sum(onehot * (cum - per_tok + pstart)[None], axis=-1)
    n_blocks = -(-tk // MOE_ROWS) + n_exp
    n_used = (pend[-1:] // MOE_ROWS).astype(jnp.int32)
    n_rows = n_blocks * MOE_ROWS
    assert n_rows == tk + n_exp * MOE_ROWS
    i = jnp.arange(MOE_ROWS, dtype=jnp.int32)[None, :]
    pad_keys = jnp.where(i < (pcounts - counts)[:, None], (pstart + counts)[:, None] + i, n_rows)
    keys = jnp.concatenate([dest.reshape(-1), pad_keys.reshape(-1)]).astype(jnp.int32)
    toks = jnp.concatenate([jnp.tile(jnp.arange(t, dtype=jnp.int32), top_k),
                            jnp.arange(n_exp * MOE_ROWS, dtype=jnp.int32) % t])
    _, row_tok = lax.sort((keys, toks), num_keys=1)
    return dest, row_tok, (pstart // MOE_ROWS).astype(jnp.int32), (pcounts // MOE_ROWS).astype(jnp.int32), n_used


def _moe(h, hb, w_router, router_bias, w_gate_t, w_up_t, w_down, ws_gate_t, ws_up_t, ws_down, g, b, alpha):
    t, d = h.shape
    n_exp = w_router.shape[1]
    idx_t, wt_t = _router(hb, w_router, router_bias.reshape(n_exp, 1))
    dest, row_tok, blk0, nblk, n_used = _dispatch(idx_t, n_exp)
    x_sorted = hb[row_tok]
    y_sorted = _experts(blk0, nblk, n_used, x_sorted, w_gate_t, w_up_t, w_down)
    yg = y_sorted[dest]
    return _final(h, hb, yg, wt_t.T, ws_gate_t, ws_up_t, ws_down, g, b, alpha)


def _rope_tables(pos):
    half = B_HEAD_DIM // 2
    inv = 1.0 / (ROPE_THETA ** (jnp.arange(half, dtype=F32) / half))
    ang = pos.astype(F32)[:, None] * inv[None, :]
    cos, sin = jnp.cos(ang), jnp.sin(ang)
    reps = LANES // B_HEAD_DIM
    return (jnp.concatenate([cos, cos] * reps, axis=1), jnp.concatenate([-sin, sin] * reps, axis=1))


def kernel(x_prompt, x_sample, mem_prompt, cache_k, cache_v, page_table, cache_mem_k, cache_mem_v, w_in, a_ln_g, a_ln_b, a_ws, a_bs, lam_q1, lam_k1, lam_q2, lam_k2, b_subln_g, w_mk, w_mv, w_pa, w_pb, w_pc, w_o, ln1_g, ln1_b, w_router, router_bias, w_gate, w_up, w_down, ws_gate, ws_up, ws_down, ln2_g, ln2_b):
    depth = w_in.shape[0]
    assert depth == 1, "single-layer step"
    layer = 0
    batch, seq, d = x_prompt.shape
    n_seq, dec, _ = x_sample.shape
    n_mem = mem_prompt.shape[1]
    n_phys, page, n_bh = cache_k.shape[1:4]
    past = page_table.shape[1] * page
    c_width = cache_mem_k.shape[3] * C_HEAD_DIM
    a_width = A_GROUPS * A_GROUP_DIM
    qk_width = n_bh * 2 * B_HEAD_DIM
    bv_width = n_bh * B_V_DIM
    assert CHUNK % dec == 0 and seq % CHUNK == 0
    alpha = (2.0 * depth) ** 0.25
    lam_init = _lambda_init(layer)
    tp, ts = batch * seq, n_seq * dec
    t = tp + ts

    x = jnp.concatenate([x_prompt.reshape(tp, d), x_sample.reshape(ts, d)], axis=0)
    xb = x.astype(BF16)
    w_in_b = w_in[layer].astype(BF16)
    o = [0, 2 * a_width]
    for wdt in (qk_width, qk_width, bv_width, c_width, 3 * d):
        o.append(o[-1] + wdt)
    pos = jnp.concatenate([jnp.tile(jnp.arange(seq), batch), past + jnp.tile(jnp.arange(dec), n_seq)])
    rope = _rope_tables(pos)

    reps = CHUNK // dec
    ws = a_ws[layer]
    ws_s = jnp.einsum("ab,gts->gatbs", jnp.eye(reps, dtype=ws.dtype), ws[:, :dec, :dec]).reshape(
        A_GROUPS, CHUNK, CHUNK)
    ws2 = jnp.stack([ws, ws_s]).astype(BF16)
    bs = a_bs[layer]
    bs2 = jnp.stack([bs, jnp.tile(bs[:, :dec], (1, reps))])
    bs2 = jnp.broadcast_to(bs2[..., None], bs2.shape + (CHUNK,)).astype(F32)

    o_a, vn = _proj_a(xb, w_in_b[:, o[0]:o[1]], a_ln_g[layer].reshape(1, a_width),
                      a_ln_b[layer].reshape(1, a_width), ws2, bs2, tp)
    q = _mm(xb, w_in_b[:, o[1]:o[2]], BF16, epilogue="rope", scale=B_HEAD_DIM ** -0.5, rope=rope)
    v = _mm(xb, w_in_b[:, o[3]:o[4]], F32)
    cq = _mm(xb, w_in_b[:, o[4]:o[5]], BF16)
    gates = _mm(xb, w_in_b[:, o[5]:o[6]], BF16, epilogue="sigmoid")
    w_k = w_in_b[:, o[2]:o[3]]
    w_k_t = w_k.T
    cos_t, sin_t = rope[0][:, :B_HEAD_DIM // 2].T, rope[1][:, B_HEAD_DIM // 2:B_HEAD_DIM].T
    kt_p = _proj_kt(w_k_t, xb, cos_t, sin_t, batch, tp, tm=512)
    xs_t = jnp.swapaxes(x_sample, 0, 1).reshape(ts, d).astype(BF16)
    pos_s = past + jnp.repeat(jnp.arange(dec), n_seq)
    cos_s, sin_s = _rope_tables(pos_s)
    kt_s = _proj_kt(w_k_t, xs_t, cos_s[:, :B_HEAD_DIM // 2].T, sin_s[:, B_HEAD_DIM // 2:B_HEAD_DIM].T, dec,
                    ts, tm=n_seq)
    k_s = _mm(xb[tp:], w_k, F32, epilogue="rope", rope=(rope[0][tp:], rope[1][tp:]))

    lamv = jnp.stack([lam_q1[layer], lam_k1[layer], lam_q2[layer], lam_k2[layer]]).astype(F32)
    g_sub = b_subln_g[layer].reshape(1, B_V_DIM)
    ob_p = _attn_prompt(lamv, q, kt_p, v, g_sub, batch, seq, n_bh, lam_init)
    cache_kt = jnp.transpose(cache_k[layer], (0, 2, 3, 4, 1)).reshape(n_phys, n_bh, 2 * B_HEAD_DIM, page)
    ob_s = _attn_sample(page_table, lamv, q[tp:].reshape(n_seq, dec, qk_width),
                        k_s.reshape(n_seq, dec, qk_width), v[tp:].reshape(n_seq, dec, bv_width), g_sub,
                        cache_kt, cache_v[layer].reshape(n_phys, page * n_bh, B_V_DIM), lam_init)
    o_b = jnp.concatenate([ob_p, ob_s.reshape(ts, bv_width)], axis=0)

    c_heads = c_width // C_HEAD_DIM
    memb = mem_prompt.reshape(batch * n_mem, d).astype(BF16)
    mk = _mm(memb, w_mk[layer].astype(BF16), F32)
    mv = _mm(memb, w_mv[layer].astype(BF16), F32)
    oc_p = _mem_attn(cq[:tp].reshape(batch, seq, c_width), mk.reshape(batch, n_mem, c_width),
                     mv.reshape(batch, n_mem, c_width), interleaved=False)
    oc_s = _mem_attn(cq[tp:].reshape(n_seq, dec, c_width),
                     cache_mem_k[layer].reshape(n_seq, n_mem * c_heads, C_HEAD_DIM),
                     cache_mem_v[layer].reshape(n_seq, n_mem * c_heads, C_HEAD_DIM), interleaved=True)
    o_c = jnp.concatenate([oc_p.reshape(tp, c_width), oc_s.reshape(ts, c_width)], axis=0)

    m = _merge(o_a, o_b, o_c, gates, w_pa[layer].astype(BF16), w_pb[layer].astype(BF16),
               w_pc[layer].astype(BF16))
    h1, h1b = _out_proj(m, w_o[layer].astype(BF16), x, ln1_g[layer].reshape(1, d), ln1_b[layer].reshape(1, d),
                        alpha)
    y = _moe(h1, h1b, w_router[layer].astype(BF16), router_bias[layer],
             jnp.swapaxes(w_gate[layer], 1, 2), jnp.swapaxes(w_up[layer], 1, 2), w_down[layer],
             ws_gate[layer].T.astype(BF16), ws_up[layer].T.astype(BF16), ws_down[layer].astype(BF16),
             ln2_g[layer].reshape(1, d), ln2_b[layer].reshape(1, d), alpha)

    k_prompt = jnp.transpose(kt_p.reshape(batch, n_bh, 2, B_HEAD_DIM, seq), (0, 4, 1, 2, 3))
    k_sample = jnp.transpose(kt_s.reshape(dec, n_bh, 2, B_HEAD_DIM, n_seq), (4, 0, 1, 2, 3))
    return (y[:tp].reshape(batch, seq, d),
            y[tp:].reshape(n_seq, dec, d),
            k_prompt[None],
            v[:tp].reshape(1, batch, seq, n_bh, B_V_DIM),
            mk.reshape(1, batch, n_mem, c_heads, C_HEAD_DIM),
            mv.reshape(1, batch, n_mem, c_heads, C_HEAD_DIM),
            k_sample[None],
            v[tp:].reshape(1, n_seq, dec, n_bh, B_V_DIM),
            vn[tp:].reshape(1, n_seq, dec, a_width))
```

```python
import functools
import math

import jax
import jax.numpy as jnp
from jax import lax
from jax.experimental import pallas as pl
from jax.experimental.pallas import tpu as pltpu

F32 = jnp.float32
BF16 = jnp.bfloat16

LN_EPS = 1e-5
ROPE_THETA = 10000.0
LANES = 128
A_GROUP_DIM = 128
A_GROUPS = 4
CHUNK = 128
B_HEAD_DIM = 64
B_V_DIM = 128
C_HEAD_DIM = 128
N_GROUPS = 8
TOPK_GROUPS = 4
TOP_K = 8
ROUTED_SCALE = 2.5
MOE_ROWS = 256
VMEM_LIMIT = 56 * 1024 * 1024


def _lambda_init(layer):
    return 0.8 - 0.6 * math.exp(-0.3 * layer)


def _params(*sem):
    return pltpu.CompilerParams(dimension_semantics=sem, vmem_limit_bytes=VMEM_LIMIT)


def _tile(n, pref):
    t = min(n, pref)
    while n % t:
        t //= 2
    return t


def _dot(a, b):
    return jnp.dot(a, b, preferred_element_type=F32)


def _dot_t(a, b):
    return lax.dot_general(a, b, (((1,), (1,)), ((), ())), preferred_element_type=F32)


def _mm_kernel(*refs, epilogue, scale):
    if epilogue == "rope":
        x_ref, w_ref, cos_ref, sin_ref, o_ref = refs
    else:
        x_ref, w_ref, o_ref = refs
    acc = _dot(x_ref[...], w_ref[...])
    if epilogue == "rope":
        n = acc.shape[1] // LANES
        cos = jnp.concatenate([cos_ref[...]] * n, axis=1)
        sin = jnp.concatenate([sin_ref[...]] * n, axis=1)
        lane = lax.broadcasted_iota(jnp.int32, acc.shape, 1)
        first = (lane % B_HEAD_DIM) < (B_HEAD_DIM // 2)
        half = B_HEAD_DIM // 2
        swapped = jnp.where(first, pltpu.roll(acc, acc.shape[1] - half, 1), pltpu.roll(acc, half, 1))
        acc = acc * cos + swapped * sin
    elif epilogue == "sigmoid":
        acc = jax.nn.sigmoid(acc)
    if scale != 1.0:
        acc = acc * scale
    o_ref[...] = acc.astype(o_ref.dtype)


def _mm(x, w, out_dtype, *, epilogue="none", scale=1.0, rope=None, tn=1024):
    m, k = x.shape
    n = w.shape[1]
    tm, tn = _tile(m, 512 if epilogue == "rope" else 1024), _tile(n, tn)
    in_specs = [pl.BlockSpec((tm, k), lambda i, j: (i, 0)),
                pl.BlockSpec((k, tn), lambda i, j: (0, j))]
    args = [x, w]
    if epilogue == "rope":
        in_specs += [pl.BlockSpec((tm, LANES), lambda i, j: (i, 0))] * 2
        args += list(rope)
    return pl.pallas_call(
        functools.partial(_mm_kernel, epilogue=epilogue, scale=scale),
        out_shape=jax.ShapeDtypeStruct((m, n), out_dtype),
        grid=(m // tm, n // tn),
        in_specs=in_specs,
        out_specs=pl.BlockSpec((tm, tn), lambda i, j: (i, j)),
        compiler_params=_params("parallel", "arbitrary"),
        name="proj_" + epilogue,
    )(*args)


def _kt_kernel(w_ref, x_ref, cos_ref, sin_ref, o_ref):
    acc = _dot_t(w_ref[...], x_ref[...])
    cos, sin = cos_ref[...], sin_ref[...]
    half = B_HEAD_DIM // 2
    for g in range(acc.shape[0] // B_HEAD_DIM):
        lo = g * B_HEAD_DIM
        x1 = acc[lo:lo + half]
        x2 = acc[lo + half:lo + B_HEAD_DIM]
        o_ref[0, lo:lo + half, :] = x1 * cos - x2 * sin
        o_ref[0, lo + half:lo + B_HEAD_DIM, :] = x2 * cos + x1 * sin


def _proj_kt(w_t, x, cos_t, sin_t, n_groups, m, *, tm):
    n, k = w_t.shape
    per = m // n_groups
    tm = _tile(per, tm)
    nt = per // tm
    half = B_HEAD_DIM // 2
    return pl.pallas_call(
        _kt_kernel,
        out_shape=jax.ShapeDtypeStruct((n_groups, n, per), F32),
        grid=(m // tm,),
        in_specs=[pl.BlockSpec((n, k), lambda i: (0, 0)),
                  pl.BlockSpec((tm, k), lambda i: (i, 0)),
                  pl.BlockSpec((half, tm), lambda i: (0, i)),
                  pl.BlockSpec((half, tm), lambda i: (0, i))],
        out_specs=pl.BlockSpec((1, n, tm), lambda i: (i // nt, 0, i % nt)),
        compiler_params=_params("parallel"),
        name="proj_kt",
    )(w_t, x, cos_t, sin_t)


def _proj_a_kernel(x_ref, w_ref, lng_ref, lnb_ref, ws_ref, bs_ref, oa_ref, vn_ref):
    h = _dot(x_ref[...], w_ref[...])
    aw = h.shape[1] // 2
    u = jax.nn.gelu(h[:, :aw])
    g = jax.nn.gelu(h[:, aw:])
    mu = jnp.mean(g, axis=-1, keepdims=True)
    var = jnp.mean(jnp.square(g - mu), axis=-1, keepdims=True)
    vn = (g - mu) * lax.rsqrt(var + LN_EPS) * lng_ref[...] + lnb_ref[...]
    vn_ref[...] = vn
    vnb = vn.astype(BF16)
    row = lax.broadcasted_iota(jnp.int32, (CHUNK, CHUNK), 0)
    col = lax.broadcasted_iota(jnp.int32, (CHUNK, CHUNK), 1)
    causal = col <= row
    for gi in range(A_GROUPS):
        wsg = jnp.where(causal, ws_ref[0, gi], jnp.zeros((), BF16))
        cols = slice(gi * A_GROUP_DIM, (gi + 1) * A_GROUP_DIM)
        for c in range(h.shape[0] // CHUNK):
            rows = slice(c * CHUNK, (c + 1) * CHUNK)
            s = _dot(wsg, vnb[rows, cols]) + bs_ref[0, gi]
            oa_ref[rows, cols] = (u[rows, cols] * s).astype(oa_ref.dtype)


def _proj_a(x, w, lng, lnb, ws2, bs2, n_prompt, *, tm=512):
    m, k = x.shape
    n = w.shape[1]
    tm = _tile(math.gcd(m, n_prompt), tm)
    npt = n_prompt // tm
    sel = lambda i: (jnp.where(i >= npt, 1, 0), 0, 0, 0)
    return pl.pallas_call(
        _proj_a_kernel,
        out_shape=(jax.ShapeDtypeStruct((m, n // 2), BF16), jax.ShapeDtypeStruct((m, n // 2), F32)),
        grid=(m // tm,),
        in_specs=[pl.BlockSpec((tm, k), lambda i: (i, 0)),
                  pl.BlockSpec((k, n), lambda i: (0, 0)),
                  pl.BlockSpec((1, n // 2), lambda i: (0, 0)),
                  pl.BlockSpec((1, n // 2), lambda i: (0, 0)),
                  pl.BlockSpec((1, A_GROUPS, CHUNK, CHUNK), sel),
                  pl.BlockSpec((1, A_GROUPS, CHUNK, CHUNK), sel)],
        out_specs=(pl.BlockSpec((tm, n // 2), lambda i: (i, 0)),
                   pl.BlockSpec((tm, n // 2), lambda i: (i, 0))),
        compiler_params=_params("parallel"),
        name="proj_a",
    )(x, w, lng, lnb, ws2, bs2)


def _diff_lambda(lam_ref, lam_init):
    lv = lam_ref[...]
    s1 = jnp.sum(lv[0:1] * lv[1:2], axis=-1, keepdims=True)
    s2 = jnp.sum(lv[2:3] * lv[3:4], axis=-1, keepdims=True)
    return jnp.exp(s1) - jnp.exp(s2) + lam_init


def _stack_maps(q):
    lane = lax.broadcasted_iota(jnp.int32, q.shape, 1)
    zero = jnp.zeros((), q.dtype)
    return jnp.concatenate([jnp.where(lane < B_HEAD_DIM, q, zero),
                            jnp.where(lane >= B_HEAD_DIM, q, zero)], axis=0)


def _sub_norm(o, g, lam_init):
    r = lax.rsqrt(jnp.mean(o * o, axis=-1, keepdims=True) + LN_EPS)
    return o * r * g * (1.0 - lam_init)


def _attn_p_kernel(lam_ref, q_ref, kt_ref, v_ref, g_ref, o_ref, kt_scr, vb_scr, *, tq, nh, lam_init):
    qi = pl.program_id(2)
    hd = B_V_DIM

    @pl.when(qi == 0)
    def _():
        for kb in range(kt_scr.shape[0]):
            kt_scr[kb] = kt_ref[0, :, kb * tq:(kb + 1) * tq].astype(BF16)
        vb_scr[...] = v_ref[...].astype(BF16)

    qs = [_stack_maps(q_ref[:, h * hd:(h + 1) * hd]) for h in range(nh)]

    def step(carry, kb, diag):
        off = pl.multiple_of(kb * tq, tq)
        kt = kt_scr[kb]
        out = []
        for h in range(nh):
            s = _dot(qs[h], kt[h * hd:(h + 1) * hd])
            if diag:
                row = lax.broadcasted_iota(jnp.int32, s.shape, 0) % tq
                col = lax.broadcasted_iota(jnp.int32, s.shape, 1)
                s = jnp.where(col <= row, s, -jnp.inf)
            m, l, acc = carry[h]
            m_new = jnp.maximum(m, jnp.max(s, axis=-1, keepdims=True))
            a = jnp.exp(m - m_new)
            p = jnp.exp(s - m_new)
            l = a * l + jnp.sum(p, axis=-1, keepdims=True)
            acc = a * acc + _dot(p.astype(BF16), vb_scr[pl.ds(off, tq), h * hd:(h + 1) * hd])
            out.append((m_new, l, acc))
        return tuple(out)

    init = tuple((jnp.full((2 * tq, 1), -jnp.inf, F32), jnp.zeros((2 * tq, 1), F32),
                  jnp.zeros((2 * tq, hd), F32)) for _ in range(nh))
    carry = lax.fori_loop(0, qi, lambda kb, c: step(c, kb, False), init)
    carry = step(carry, qi, True)
    lam = _diff_lambda(lam_ref, lam_init)
    for h in range(nh):
        _, l, acc = carry[h]
        o = acc / l
        o = o[:tq] - lam * o[tq:]
        o_ref[:, h * hd:(h + 1) * hd] = _sub_norm(o, g_ref[...], lam_init).astype(o_ref.dtype)


def _attn_prompt(lamv, q, kt, v, g, batch, seq, n_heads, lam_init, *, tq=1024, nh=2):
    tq = _tile(seq, tq)
    nq = seq // tq
    nh = math.gcd(nh, n_heads)
    w = nh * B_V_DIM
    return pl.pallas_call(
        functools.partial(_attn_p_kernel, tq=tq, nh=nh, lam_init=lam_init),
        out_shape=jax.ShapeDtypeStruct((batch * seq, n_heads * B_V_DIM), BF16),
        grid=(batch, n_heads // nh, nq),
        in_specs=[pl.BlockSpec(lamv.shape, lambda b, h, i: (0, 0)),
                  pl.BlockSpec((tq, w), lambda b, h, i: (b * nq + i, h)),
                  pl.BlockSpec((1, w, seq), lambda b, h, i: (b, h, 0)),
                  pl.BlockSpec((seq, w), lambda b, h, i: (b, h)),
                  pl.BlockSpec((1, B_V_DIM), lambda b, h, i: (0, 0))],
        out_specs=pl.BlockSpec((tq, w), lambda b, h, i: (b * nq + i, h)),
        scratch_shapes=[pltpu.VMEM((nq, w, tq), BF16), pltpu.VMEM((seq, w), BF16)],
        compiler_params=_params("parallel", "parallel", "arbitrary"),
        name="attn_prompt",
    )(lamv, q, kt, v, g)


def _attn_s_kernel(pt_ref, lam_ref, q_ref, kn_ref, vn_ref, g_ref, *refs, n_pages, n_heads, page, dec,
                   lam_init):
    kp = refs[:n_pages]
    vp = refs[n_pages:2 * n_pages]
    o_ref = refs[2 * n_pages]
    s_scr, kn_scr, vn_scr = refs[2 * n_pages + 1:]

    @pl.when(pl.program_id(0) == 0)
    def _():
        kn_scr[...] = jnp.zeros_like(kn_scr)
        vn_scr[...] = jnp.zeros_like(vn_scr)

    kn_scr[0:dec, :] = kn_ref[0]
    vn_scr[0:dec, :] = vn_ref[0]
    lam = _diff_lambda(lam_ref, lam_init)
    past = n_pages * page
    for h in range(n_heads):
        cols = slice(h * B_V_DIM, (h + 1) * B_V_DIM)
        qs = _stack_maps(q_ref[0, :, cols])
        for p in range(n_pages):
            s_scr[:, p * page:(p + 1) * page] = _dot(qs, kp[p][0, h].astype(BF16))
        sn = _dot_t(qs, kn_scr[:, cols].astype(BF16))
        row = lax.broadcasted_iota(jnp.int32, sn.shape, 0) % dec
        col = lax.broadcasted_iota(jnp.int32, sn.shape, 1)
        s_scr[:, past:] = jnp.where(col <= row, sn, -jnp.inf)
        s = s_scr[...]
        m = jnp.max(s, axis=-1, keepdims=True)
        e = jnp.exp(s - m)
        pr = (e / jnp.sum(e, axis=-1, keepdims=True)).astype(BF16)
        o = _dot(pr[:, past:], vn_scr[:, cols].astype(BF16))
        for p in range(n_pages):
            vh = vp[p][0, pl.ds(h, page, stride=n_heads), :]
            o = o + _dot(pr[:, p * page:(p + 1) * page], vh.astype(BF16))
        o = o[:dec] - lam * o[dec:]
        o_ref[0, :, cols] = _sub_norm(o, g_ref[...], lam_init).astype(o_ref.dtype)


def _attn_sample(page_table, lamv, q, k_new, v_new, g, cache_kt, cache_v, lam_init):
    n_seq, dec, width = q.shape
    n_pages = page_table.shape[1]
    n_heads = width // B_V_DIM
    page = cache_kt.shape[3]
    tok = lambda b, pt: (b, 0, 0)
    const = lambda b, pt: (0, 0)

    def k_spec(p):
        return pl.BlockSpec((1, n_heads, B_V_DIM, page), lambda b, pt: (pt[b, p], 0, 0, 0))

    def v_spec(p):
        return pl.BlockSpec((1, page * n_heads, B_V_DIM), lambda b, pt: (pt[b, p], 0, 0))

    grid_spec = pltpu.PrefetchScalarGridSpec(
        num_scalar_prefetch=1,
        grid=(n_seq,),
        in_specs=[pl.BlockSpec(lamv.shape, const),
                  pl.BlockSpec((1, dec, width), tok),
                  pl.BlockSpec((1, dec, width), tok),
                  pl.BlockSpec((1, dec, width), tok),
                  pl.BlockSpec((1, B_V_DIM), const)]
                 + [k_spec(p) for p in range(n_pages)]
                 + [v_spec(p) for p in range(n_pages)],
        out_specs=pl.BlockSpec((1, dec, width), tok),
        scratch_shapes=[pltpu.VMEM((2 * dec, (n_pages + 1) * page), F32),
                        pltpu.VMEM((page, width), F32),
                        pltpu.VMEM((page, width), F32)],
    )
    return pl.pallas_call(
        functools.partial(_attn_s_kernel, n_pages=n_pages, n_heads=n_heads, page=page, dec=dec,
                          lam_init=lam_init),
        out_shape=jax.ShapeDtypeStruct((n_seq, dec, width), BF16),
        grid_spec=grid_spec,
        compiler_params=_params("arbitrary"),
        name="attn_sample",
    )(page_table, lamv, q, k_new, v_new, g, *([cache_kt] * n_pages), *([cache_v] * n_pages))


def _mem_attn_kernel(q_ref, k_ref, v_ref, o_ref, *, n_heads, interleaved):
    tq = q_ref.shape[1]
    pad = (-tq) % 16
    n_mem = k_ref.shape[1] // n_heads if interleaved else k_ref.shape[1]
    for h in range(n_heads):
        cols = slice(h * C_HEAD_DIM, (h + 1) * C_HEAD_DIM)
        if interleaved:
            kh = k_ref[0, pl.ds(h, n_mem, stride=n_heads), :]
            vh = v_ref[0, pl.ds(h, n_mem, stride=n_heads), :]
        else:
            kh, vh = k_ref[0, :, cols], v_ref[0, :, cols]
        q = q_ref[0, :, cols]
        if pad:
            q = jnp.concatenate([q, jnp.zeros((pad, C_HEAD_DIM), q.dtype)], axis=0)
        s = _dot_t(q, kh.astype(BF16)) * (C_HEAD_DIM ** -0.5)
        m = jnp.max(s, axis=-1, keepdims=True)
        e = jnp.exp(s - m)
        pr = (e / jnp.sum(e, axis=-1, keepdims=True)).astype(BF16)
        o = _dot(pr, vh.astype(BF16))
        o_ref[0, :, cols] = o[:tq].astype(o_ref.dtype)


def _mem_attn(q, mk, mv, *, interleaved, tq=512):
    b, t, w = q.shape
    tq = _tile(t, tq)
    kv_spec = pl.BlockSpec((1,) + mk.shape[1:], lambda i, j: (i, 0, 0))
    return pl.pallas_call(
        functools.partial(_mem_attn_kernel, n_heads=w // C_HEAD_DIM, interleaved=interleaved),
        out_shape=jax.ShapeDtypeStruct((b, t, w), BF16),
        grid=(b, t // tq),
        in_specs=[pl.BlockSpec((1, tq, w), lambda i, j: (i, j, 0)), kv_spec, kv_spec],
        out_specs=pl.BlockSpec((1, tq, w), lambda i, j: (i, j, 0)),
        compiler_params=_params("parallel", "arbitrary"),
        name="mem_attn",
    )(q, mk, mv)


def _merge_kernel(oa_ref, ob_ref, oc_ref, ga_ref, gb_ref, gc_ref, wa_ref, wb_ref, wc_ref, m_ref):
    m = ga_ref[...].astype(F32) * _dot(oa_ref[...], wa_ref[...])
    m = m + gb_ref[...].astype(F32) * _dot(ob_ref[...], wb_ref[...])
    m = m + gc_ref[...].astype(F32) * _dot(oc_ref[...], wc_ref[...])
    m_ref[...] = m.astype(m_ref.dtype)


def _merge(oa, ob, oc, gates, wa, wb, wc, *, tm=1024, tn=512):
    t = oa.shape[0]
    d = wa.shape[1]
    tm, tn = _tile(t, tm), _tile(d, tn)
    nj = d // tn
    row = lambda width: pl.BlockSpec((tm, width), lambda i, j: (i, 0))
    gate = lambda b: pl.BlockSpec((tm, tn), lambda i, j: (i, b * nj + j))
    wcol = lambda width: pl.BlockSpec((width, tn), lambda i, j: (0, j))
    return pl.pallas_call(
        _merge_kernel,
        out_shape=jax.ShapeDtypeStruct((t, d), BF16),
        grid=(t // tm, nj),
        in_specs=[row(oa.shape[1]), row(ob.shape[1]), row(oc.shape[1]), gate(0), gate(1), gate(2),
                  wcol(wa.shape[0]), wcol(wb.shape[0]), wcol(wc.shape[0])],
        out_specs=pl.BlockSpec((tm, tn), lambda i, j: (i, j)),
        compiler_params=_params("parallel", "arbitrary"),
        name="merge",
    )(oa, ob, oc, gates, gates, gates, wa, wb, wc)


def _layer_norm(x, g, b):
    mu = jnp.mean(x, axis=-1, keepdims=True)
    var = jnp.mean(jnp.square(x - mu), axis=-1, keepdims=True)
    return (x - mu) * lax.rsqrt(var + LN_EPS) * g + b


def _pack_pairs(x):
    c = x.shape[1] // 2
    bits = lax.bitcast_convert_type(x.astype(BF16).astype(F32), jnp.uint32)
    return bits[:, c:] | (bits[:, :c] >> 16)


def _unpack_pairs(w):
    lo = lax.bitcast_convert_type(w << 16, F32)
    hi = lax.bitcast_convert_type(w & jnp.uint32(0xFFFF0000), F32)
    return jnp.concatenate([lo, hi], axis=1)


def _out_kernel(m_ref, w_ref, x_ref, g_ref, b_ref, h_ref, hb_ref, hp_ref, *, alpha):
    y = alpha * x_ref[...] + _dot(m_ref[...], w_ref[...])
    h = _layer_norm(y, g_ref[...], b_ref[...])
    h_ref[...] = h
    hb_ref[...] = h.astype(BF16)
    hp_ref[...] = _pack_pairs(h)


def _out_proj(m, w_o, x, g, b, alpha, *, tm=512):
    t, d = x.shape
    tm = _tile(t, tm)
    row = pl.BlockSpec((tm, d), lambda i: (i, 0))
    half = pl.BlockSpec((tm, d // 2), lambda i: (i, 0))
    vec = pl.BlockSpec((1, d), lambda i: (0, 0))
    return pl.pallas_call(
        functools.partial(_out_kernel, alpha=alpha),
        out_shape=(jax.ShapeDtypeStruct((t, d), F32), jax.ShapeDtypeStruct((t, d), BF16),
                   jax.ShapeDtypeStruct((t, d // 2), jnp.uint32)),
        grid=(t // tm,),
        in_specs=[row, pl.BlockSpec((d, d), lambda i: (0, 0)), row, vec, vec],
        out_specs=(row, row, half),
        compiler_params=_params("parallel"),
        name="out_proj_ln1",
    )(m, w_o, x, g, b)


def _first_argmax(vals, idx, sentinel):
    m = jnp.max(vals, axis=0, keepdims=True)
    first = jnp.min(jnp.where(vals == m, idx, sentinel), axis=0, keepdims=True)
    return m, first


def _router_kernel(x_ref, w_ref, bias_ref, idx_ref, wt_ref):
    s = jax.nn.sigmoid(_dot(x_ref[...], w_ref[...]))
    st = s.T
    sb = st + bias_ref[...]
    n_exp, tm = st.shape
    gs = n_exp // N_GROUPS
    neg = -jnp.inf
    ridx = lax.broadcasted_iota(jnp.int32, (gs, tm), 0)
    grp = []
    for g in range(N_GROUPS):
        blk = sb[g * gs:(g + 1) * gs]
        m1, first = _first_argmax(blk, ridx, gs)
        m2 = jnp.max(jnp.where(ridx == first, neg, blk), axis=0, keepdims=True)
        grp.append(m1 + m2)
    work = jnp.concatenate(grp, axis=0)
    gidx = lax.broadcasted_iota(jnp.int32, work.shape, 0)
    gmask = jnp.zeros(work.shape, jnp.bool_)
    for _ in range(TOPK_GROUPS):
        _, first = _first_argmax(work, gidx, N_GROUPS)
        sel = gidx == first
        gmask = jnp.logical_or(gmask, sel)
        work = jnp.where(sel, neg, work)
    masked = jnp.concatenate(
        [jnp.where(gmask[g:g + 1], sb[g * gs:(g + 1) * gs], neg) for g in range(N_GROUPS)], axis=0)
    eidx = lax.broadcasted_iota(jnp.int32, masked.shape, 0)
    ids, wts = [], []
    for _ in range(TOP_K):
        _, first = _first_argmax(masked, eidx, n_exp)
        sel = eidx == first
        ids.append(first)
        wts.append(jnp.sum(jnp.where(sel, st, 0.0), axis=0, keepdims=True))
        masked = jnp.where(sel, neg, masked)
    wt = jnp.concatenate(wts, axis=0)
    idx_ref[...] = jnp.concatenate(ids, axis=0)
    wt_ref[...] = wt / jnp.sum(wt, axis=0, keepdims=True) * ROUTED_SCALE


def _router(hb, w_router, bias, *, tm=256):
    t, d = hb.shape
    n_exp = w_router.shape[1]
    tm = _tile(t, tm)
    return pl.pallas_call(
        _router_kernel,
        out_shape=(jax.ShapeDtypeStruct((TOP_K, t), jnp.int32), jax.ShapeDtypeStruct((TOP_K, t), F32)),
        grid=(t // tm,),
        in_specs=[pl.BlockSpec((tm, d), lambda i: (i, 0)),
                  pl.BlockSpec((d, n_exp), lambda i: (0, 0)),
                  pl.BlockSpec((n_exp, 1), lambda i: (0, 0))],
        out_specs=(pl.BlockSpec((TOP_K, tm), lambda i: (0, i)),
                   pl.BlockSpec((TOP_K, tm), lambda i: (0, i))),
        compiler_params=_params("parallel"),
        name="router",
    )(hb, w_router, bias)


def _expert_kernel(b0_ref, nb_ref, nu_ref, x_hbm, wg_hbm, wu_hbm, wd_hbm, y_hbm, xbuf, ybuf, wbuf, sem_in, sem_out,
                   sem_w, w1_scr, wd_scr, *, f, up_off, n_blocks):
    e = pl.program_id(0)
    last = pl.num_programs(0) - 1
    b0 = b0_ref[e]
    nb = nb_ref[e]
    n_used = nu_ref[0]

    def x_copy(blk):
        slot = blk % 2
        return pltpu.make_async_copy(x_hbm.at[pl.ds(blk * MOE_ROWS, MOE_ROWS)], xbuf.at[slot], sem_in.at[slot])

    def y_copy(blk):
        slot = blk % 2
        return pltpu.make_async_copy(ybuf.at[slot], y_hbm.at[pl.ds(blk * MOE_ROWS, MOE_ROWS)], sem_out.at[slot])

    def w_copies(ex):
        slot = ex % 2
        return [pltpu.make_async_copy(src.at[ex], wbuf.at[slot, j], sem_w.at[slot, j])
                for j, src in enumerate((wg_hbm, wu_hbm, wd_hbm))]

    @pl.when(e == 0)
    def _():
        for cp in w_copies(0):
            cp.start(priority=1)
        w1_scr[f:up_off, :] = jnp.zeros((up_off - f, w1_scr.shape[1]), BF16)

        @pl.when(n_used > 0)
        def _():
            x_copy(0).start()

    for cp in w_copies(e):
        cp.wait()

    @pl.when(e < last)
    def _():
        for cp in w_copies(e + 1):
            cp.start(priority=1)

    @pl.when(nb > 0)
    def _():
        wslot = e % 2
        w1_scr[0:f, :] = wbuf[wslot, 0].astype(BF16)
        w1_scr[up_off:up_off + f, :] = wbuf[wslot, 1].astype(BF16)
        wd_scr[...] = wbuf[wslot, 2].astype(BF16)

        def chunk(c, carry):
            g = b0 + c
            slot = g % 2
            x_copy(g).wait()

            @pl.when(g + 1 < n_used)
            def _():
                x_copy(g + 1).start()

            @pl.when(g >= 2)
            def _():
                y_copy(g - 2).wait()

            x = _unpack_pairs(xbuf[slot]).astype(BF16)
            h = _dot_t(x, w1_scr[...])
            a = jax.nn.silu(h[:, :f]) * h[:, up_off:up_off + f]
            ybuf[slot] = _pack_pairs(_dot(a.astype(BF16), wd_scr[...]))
            y_copy(g).start()
            return carry

        lax.fori_loop(0, nb, chunk, 0)

    @pl.when(e == last)
    def _():
        @pl.when(n_used >= 2)
        def _():
            y_copy(n_used - 2).wait()

        @pl.when(n_used >= 1)
        def _():
            y_copy(n_used - 1).wait()

        ybuf[0] = jnp.zeros(ybuf.shape[1:], ybuf.dtype)

        def fill(blk, carry):
            cp = pltpu.make_async_copy(ybuf.at[0], y_hbm.at[pl.ds(blk * MOE_ROWS, MOE_ROWS)], sem_out.at[0])
            cp.start()
            cp.wait()
            return carry

        lax.fori_loop(n_used, n_blocks, fill, 0)


def _experts(blk0, nblk, n_used, x_sorted, w_gate_t, w_up_t, w_down):
    n_rows = x_sorted.shape[0]
    n_exp, f, d = w_down.shape
    dp = d // 2
    up_off = -(-f // LANES) * LANES
    n_blocks = n_rows // MOE_ROWS
    hbm = pl.BlockSpec(memory_space=pl.ANY)
    grid_spec = pltpu.PrefetchScalarGridSpec(
        num_scalar_prefetch=3,
        grid=(n_exp,),
        in_specs=[hbm, hbm, hbm, hbm],
        out_specs=hbm,
        scratch_shapes=[pltpu.VMEM((2, MOE_ROWS, dp), jnp.uint32), pltpu.VMEM((2, MOE_ROWS, dp), jnp.uint32),
                        pltpu.VMEM((2, 3, f, d), F32),
                        pltpu.SemaphoreType.DMA((2,)), pltpu.SemaphoreType.DMA((2,)),
                        pltpu.SemaphoreType.DMA((2, 3)),
                        pltpu.VMEM((up_off + f, d), BF16), pltpu.VMEM((f, d), BF16)],
    )
    return pl.pallas_call(
        functools.partial(_expert_kernel, f=f, up_off=up_off, n_blocks=n_blocks),
        out_shape=jax.ShapeDtypeStruct((n_rows, dp), jnp.uint32),
        grid_spec=grid_spec,
        compiler_params=_params("arbitrary"),
        name="experts",
    )(blk0, nblk, n_used, x_sorted, w_gate_t, w_up_t, w_down)


def _final_kernel(h_ref, hb_ref, yg_ref, wt_ref, wg_ref, wu_ref, wd_ref, g_ref, b_ref, o_ref, *, alpha):
    hb = hb_ref[...]
    a = jax.nn.silu(_dot_t(hb, wg_ref[...])) * _dot_t(hb, wu_ref[...])
    shared = _dot(a.astype(BF16), wd_ref[...])
    wt = wt_ref[...]
    routed = wt[:, 0:1] * _unpack_pairs(yg_ref[0])
    for k in range(1, yg_ref.shape[0]):
        routed = routed + wt[:, k:k + 1] * _unpack_pairs(yg_ref[k])
    y = alpha * h_ref[...] + (routed + shared)
    o_ref[...] = _layer_norm(y, g_ref[...], b_ref[...])


def _final(h, hb, yg, wt, ws_gate_t, ws_up_t, ws_down, g, b, alpha, *, tm=256):
    t, d = h.shape
    f = ws_down.shape[0]
    tm = _tile(t, tm)
    row = pl.BlockSpec((tm, d), lambda i: (i, 0))
    vec = pl.BlockSpec((1, d), lambda i: (0, 0))
    wfull = pl.BlockSpec((f, d), lambda i: (0, 0))
    return pl.pallas_call(
        functools.partial(_final_kernel, alpha=alpha),
        out_shape=jax.ShapeDtypeStruct((t, d), F32),
        grid=(t // tm,),
        in_specs=[row, row, pl.BlockSpec((yg.shape[0], tm, d // 2), lambda i: (0, i, 0)),
                  pl.BlockSpec((tm, wt.shape[1]), lambda i: (i, 0)), wfull, wfull, wfull, vec, vec],
        out_specs=row,
        compiler_params=_params("parallel"),
        name="shared_combine_ln2",
    )(h, hb, yg, wt, ws_gate_t, ws_up_t, ws_down, g, b)


def _dispatch(idx_t, n_exp):
    top_k, t = idx_t.shape
    tk = t * top_k
    onehot = (idx_t[:, :, None] == jnp.arange(n_exp, dtype=jnp.int32)).astype(jnp.int32)
    per_tok = jnp.sum(onehot, axis=0)
    cum = jnp.cumsum(per_tok, axis=0)
    counts = cum[-1]
    pcounts = (counts + MOE_ROWS - 1) // MOE_ROWS * MOE_ROWS
    pend = jnp.cumsum(pcounts)
    pstart = pend - pcounts
    dest = jnp.sum(onehot * (cum - per_tok + pstart)[None], axis=-1)
    n_blocks = -(-tk // MOE_ROWS) + n_exp
    n_used = (pend[-1:] // MOE_ROWS).astype(jnp.int32)
    n_rows = n_blocks * MOE_ROWS
    assert n_rows == tk + n_exp * MOE_ROWS
    i = jnp.arange(MOE_ROWS, dtype=jnp.int32)[None, :]
    pad_keys = jnp.where(i < (pcounts - counts)[:, None], (pstart + counts)[:, None] + i, n_rows)
    keys = jnp.concatenate([dest.reshape(-1), pad_keys.reshape(-1)]).astype(jnp.int32)
    toks = jnp.concatenate([jnp.tile(jnp.arange(t, dtype=jnp.int32), top_k),
                            jnp.arange(n_exp * MOE_ROWS, dtype=jnp.int32) % t])
    _, row_tok = lax.sort((keys, toks), num_keys=1)
    return dest, row_tok, (pstart // MOE_ROWS).astype(jnp.int32), (pcounts // MOE_ROWS).astype(jnp.int32), n_used


def _moe(h, hb, hp, w_router, router_bias, w_gate_t, w_up_t, w_down, ws_gate_t, ws_up_t, ws_down, g, b, alpha):
    n_exp = w_router.shape[1]
    idx_t, wt_t = _router(hb, w_router, router_bias.reshape(n_exp, 1))
    dest, row_tok, blk0, nblk, n_used = _dispatch(idx_t, n_exp)
    x_sorted = hp[row_tok]
    y_sorted = _experts(blk0, nblk, n_used, x_sorted, w_gate_t, w_up_t, w_down)
    yg = y_sorted[dest]
    return _final(h, hb, yg, wt_t.T, ws_gate_t, ws_up_t, ws_down, g, b, alpha)


def _rope_tables(pos):
    half = B_HEAD_DIM // 2
    inv = 1.0 / (ROPE_THETA ** (jnp.arange(half, dtype=F32) / half))
    ang = pos.astype(F32)[:, None] * inv[None, :]
    cos, sin = jnp.cos(ang), jnp.sin(ang)
    reps = LANES // B_HEAD_DIM
    return (jnp.concatenate([cos, cos] * reps, axis=1), jnp.concatenate([-sin, sin] * reps, axis=1))


def kernel(x_prompt, x_sample, mem_prompt, cache_k, cache_v, page_table, cache_mem_k, cache_mem_v, w_in, a_ln_g, a_ln_b, a_ws, a_bs, lam_q1, lam_k1, lam_q2, lam_k2, b_subln_g, w_mk, w_mv, w_pa, w_pb, w_pc, w_o, ln1_g, ln1_b, w_router, router_bias, w_gate, w_up, w_down, ws_gate, ws_up, ws_down, ln2_g, ln2_b):
    depth = w_in.shape[0]
    assert depth == 1, "single-layer step"
    layer = 0
    batch, seq, d = x_prompt.shape
    n_seq, dec, _ = x_sample.shape
    n_mem = mem_prompt.shape[1]
    n_phys, page, n_bh = cache_k.shape[1:4]
    past = page_table.shape[1] * page
    c_width = cache_mem_k.shape[3] * C_HEAD_DIM
    a_width = A_GROUPS * A_GROUP_DIM
    qk_width = n_bh * 2 * B_HEAD_DIM
    bv_width = n_bh * B_V_DIM
    assert CHUNK % dec == 0 and seq % CHUNK == 0
    alpha = (2.0 * depth) ** 0.25
    lam_init = _lambda_init(layer)
    tp, ts = batch * seq, n_seq * dec
    t = tp + ts

    x = jnp.concatenate([x_prompt.reshape(tp, d), x_sample.reshape(ts, d)], axis=0)
    xb = x.astype(BF16)
    w_in_b = w_in[layer].astype(BF16)
    o = [0, 2 * a_width]
    for wdt in (qk_width, qk_width, bv_width, c_width, 3 * d):
        o.append(o[-1] + wdt)
    pos = jnp.concatenate([jnp.tile(jnp.arange(seq), batch), past + jnp.tile(jnp.arange(dec), n_seq)])
    rope = _rope_tables(pos)

    reps = CHUNK // dec
    ws = a_ws[layer]
    ws_s = jnp.einsum("ab,gts->gatbs", jnp.eye(reps, dtype=ws.dtype), ws[:, :dec, :dec]).reshape(
        A_GROUPS, CHUNK, CHUNK)
    ws2 = jnp.stack([ws, ws_s]).astype(BF16)
    bs = a_bs[layer]
    bs2 = jnp.stack([bs, jnp.tile(bs[:, :dec], (1, reps))])
    bs2 = jnp.broadcast_to(bs2[..., None], bs2.shape + (CHUNK,)).astype(F32)

    o_a, vn = _proj_a(xb, w_in_b[:, o[0]:o[1]], a_ln_g[layer].reshape(1, a_width),
                      a_ln_b[layer].reshape(1, a_width), ws2, bs2, tp)
    q = _mm(xb, w_in_b[:, o[1]:o[2]], BF16, epilogue="rope", scale=B_HEAD_DIM ** -0.5, rope=rope)
    v = _mm(xb, w_in_b[:, o[3]:o[4]], F32)
    cq = _mm(xb, w_in_b[:, o[4]:o[5]], BF16)
    gates = _mm(xb, w_in_b[:, o[5]:o[6]], BF16, epilogue="sigmoid")
    w_k = w_in_b[:, o[2]:o[3]]
    w_k_t = w_k.T
    cos_t, sin_t = rope[0][:, :B_HEAD_DIM // 2].T, rope[1][:, B_HEAD_DIM // 2:B_HEAD_DIM].T
    kt_p = _proj_kt(w_k_t, xb, cos_t, sin_t, batch, tp, tm=512)
    xs_t = jnp.swapaxes(x_sample, 0, 1).reshape(ts, d).astype(BF16)
    pos_s = past + jnp.repeat(jnp.arange(dec), n_seq)
    cos_s, sin_s = _rope_tables(pos_s)
    kt_s = _proj_kt(w_k_t, xs_t, cos_s[:, :B_HEAD_DIM // 2].T, sin_s[:, B_HEAD_DIM // 2:B_HEAD_DIM].T, dec,
                    ts, tm=n_seq)
    k_s = _mm(xb[tp:], w_k, F32, epilogue="rope", rope=(rope[0][tp:], rope[1][tp:]))

    lamv = jnp.stack([lam_q1[layer], lam_k1[layer], lam_q2[layer], lam_k2[layer]]).astype(F32)
    g_sub = b_subln_g[layer].reshape(1, B_V_DIM)
    ob_p = _attn_prompt(lamv, q, kt_p, v, g_sub, batch, seq, n_bh, lam_init)
    cache_kt = jnp.transpose(cache_k[layer], (0, 2, 3, 4, 1)).reshape(n_phys, n_bh, 2 * B_HEAD_DIM, page)
    ob_s = _attn_sample(page_table, lamv, q[tp:].reshape(n_seq, dec, qk_width),
                        k_s.reshape(n_seq, dec, qk_width), v[tp:].reshape(n_seq, dec, bv_width), g_sub,
                        cache_kt, cache_v[layer].reshape(n_phys, page * n_bh, B_V_DIM), lam_init)
    o_b = jnp.concatenate([ob_p, ob_s.reshape(ts, bv_width)], axis=0)

    c_heads = c_width // C_HEAD_DIM
    memb = mem_prompt.reshape(batch * n_mem, d).astype(BF16)
    mk = _mm(memb, w_mk[layer].astype(BF16), F32)
    mv = _mm(memb, w_mv[layer].astype(BF16), F32)
    oc_p = _mem_attn(cq[:tp].reshape(batch, seq, c_width), mk.reshape(batch, n_mem, c_width),
                     mv.reshape(batch, n_mem, c_width), interleaved=False)
    oc_s = _mem_attn(cq[tp:].reshape(n_seq, dec, c_width),
                     cache_mem_k[layer].reshape(n_seq, n_mem * c_heads, C_HEAD_DIM),
                     cache_mem_v[layer].reshape(n_seq, n_mem * c_heads, C_HEAD_DIM), interleaved=True)
    o_c = jnp.concatenate([oc_p.reshape(tp, c_width), oc_s.reshape(ts, c_width)], axis=0)

    m = _merge(o_a, o_b, o_c, gates, w_pa[layer].astype(BF16), w_pb[layer].astype(BF16),
               w_pc[layer].astype(BF16))
    h1, h1b, h1p = _out_proj(m, w_o[layer].astype(BF16), x, ln1_g[layer].reshape(1, d), ln1_b[layer].reshape(1, d),
                        alpha)
    y = _moe(h1, h1b, h1p, w_router[layer].astype(BF16), router_bias[layer],
             jnp.swapaxes(w_gate[layer], 1, 2), jnp.swapaxes(w_up[layer], 1, 2), w_down[layer],
             ws_gate[layer].T.astype(BF16), ws_up[layer].T.astype(BF16), ws_down[layer].astype(BF16),
             ln2_g[layer].reshape(1, d), ln2_b[layer].reshape(1, d), alpha)

    k_prompt = jnp.transpose(kt_p.reshape(batch, n_bh, 2, B_HEAD_DIM, seq), (0, 4, 1, 2, 3))
    k_sample = jnp.transpose(kt_s.reshape(dec, n_bh, 2, B_HEAD_DIM, n_seq), (4, 0, 1, 2, 3))
    return (y[:tp].reshape(batch, seq, d),
            y[tp:].reshape(n_seq, dec, d),
            k_prompt[None],
            v[:tp].reshape(1, batch, seq, n_bh, B_V_DIM),
            mk.reshape(1, batch, n_mem, c_heads, C_HEAD_DIM),
            mv.reshape(1, batch, n_mem, c_heads, C_HEAD_DIM),
            k_sample[None],
            v[tp:].reshape(1, n_seq, dec, n_bh, B_V_DIM),
            vn[tp:].reshape(1, n_seq, dec, a_width))
```

```python
import functools
import math

import jax
import jax.numpy as jnp
from jax import lax
from jax.experimental import pallas as pl
from jax.experimental.pallas import tpu as pltpu

F32 = jnp.float32
BF16 = jnp.bfloat16

LN_EPS = 1e-5
ROPE_THETA = 10000.0
LANES = 128
A_GROUP_DIM = 128
A_GROUPS = 4
CHUNK = 128
B_HEAD_DIM = 64
B_V_DIM = 128
C_HEAD_DIM = 128
N_GROUPS = 8
TOPK_GROUPS = 4
TOP_K = 8
ROUTED_SCALE = 2.5
MOE_ROWS = 256
X_SLOTS = 4
VMEM_LIMIT = 56 * 1024 * 1024


def _lambda_init(layer):
    return 0.8 - 0.6 * math.exp(-0.3 * layer)


def _params(*sem):
    return pltpu.CompilerParams(dimension_semantics=sem, vmem_limit_bytes=VMEM_LIMIT)


def _tile(n, pref):
    t = min(n, pref)
    while n % t:
        t //= 2
    return t


def _dot(a, b):
    return jnp.dot(a, b, preferred_element_type=F32)


def _dot_t(a, b):
    return lax.dot_general(a, b, (((1,), (1,)), ((), ())), preferred_element_type=F32)


def _mm_kernel(*refs, epilogue, scale):
    if epilogue == "rope":
        x_ref, w_ref, cos_ref, sin_ref, o_ref = refs
    else:
        x_ref, w_ref, o_ref = refs
    acc = _dot(x_ref[...], w_ref[...])
    if epilogue == "rope":
        n = acc.shape[1] // LANES
        cos = jnp.concatenate([cos_ref[...]] * n, axis=1)
        sin = jnp.concatenate([sin_ref[...]] * n, axis=1)
        lane = lax.broadcasted_iota(jnp.int32, acc.shape, 1)
        first = (lane % B_HEAD_DIM) < (B_HEAD_DIM // 2)
        half = B_HEAD_DIM // 2
        swapped = jnp.where(first, pltpu.roll(acc, acc.shape[1] - half, 1), pltpu.roll(acc, half, 1))
        acc = acc * cos + swapped * sin
    elif epilogue == "sigmoid":
        acc = jax.nn.sigmoid(acc)
    if scale != 1.0:
        acc = acc * scale
    o_ref[...] = acc.astype(o_ref.dtype)


def _mm(x, w, out_dtype, *, epilogue="none", scale=1.0, rope=None, tn=1024, m=None):
    k = x.shape[1]
    m = x.shape[0] if m is None else m
    n = w.shape[1]
    tm, tn = _tile(m, 512 if epilogue == "rope" else 1024), _tile(n, tn)
    in_specs = [pl.BlockSpec((tm, k), lambda i, j: (i, 0)),
                pl.BlockSpec((k, tn), lambda i, j: (0, j))]
    args = [x, w]
    if epilogue == "rope":
        in_specs += [pl.BlockSpec((tm, LANES), lambda i, j: (i, 0))] * 2
        args += list(rope)
    return pl.pallas_call(
        functools.partial(_mm_kernel, epilogue=epilogue, scale=scale),
        out_shape=jax.ShapeDtypeStruct((m, n), out_dtype),
        grid=(m // tm, n // tn),
        in_specs=in_specs,
        out_specs=pl.BlockSpec((tm, tn), lambda i, j: (i, j)),
        compiler_params=_params("parallel", "arbitrary"),
        name="proj_" + epilogue,
    )(*args)


def _kt_kernel(w_ref, x_ref, cos_ref, sin_ref, o_ref):
    acc = _dot_t(w_ref[...], x_ref[...])
    cos, sin = cos_ref[...], sin_ref[...]
    half = B_HEAD_DIM // 2
    for g in range(acc.shape[0] // B_HEAD_DIM):
        lo = g * B_HEAD_DIM
        x1 = acc[lo:lo + half]
        x2 = acc[lo + half:lo + B_HEAD_DIM]
        o_ref[0, lo:lo + half, :] = x1 * cos - x2 * sin
        o_ref[0, lo + half:lo + B_HEAD_DIM, :] = x2 * cos + x1 * sin


def _proj_kt(w_t, x, cos_t, sin_t, n_groups, m, *, tm):
    n, k = w_t.shape
    per = m // n_groups
    tm = _tile(per, tm)
    nt = per // tm
    half = B_HEAD_DIM // 2
    return pl.pallas_call(
        _kt_kernel,
        out_shape=jax.ShapeDtypeStruct((n_groups, n, per), F32),
        grid=(m // tm,),
        in_specs=[pl.BlockSpec((n, k), lambda i: (0, 0)),
                  pl.BlockSpec((tm, k), lambda i: (i, 0)),
                  pl.BlockSpec((half, tm), lambda i: (0, i)),
                  pl.BlockSpec((half, tm), lambda i: (0, i))],
        out_specs=pl.BlockSpec((1, n, tm), lambda i: (i // nt, 0, i % nt)),
        compiler_params=_params("parallel"),
        name="proj_kt",
    )(w_t, x, cos_t, sin_t)


def _proj_a_kernel(x_ref, w_ref, lng_ref, lnb_ref, ws_ref, bs_ref, oa_ref, vn_ref):
    h = _dot(x_ref[...], w_ref[...])
    aw = h.shape[1] // 2
    u = jax.nn.gelu(h[:, :aw])
    g = jax.nn.gelu(h[:, aw:])
    mu = jnp.mean(g, axis=-1, keepdims=True)
    var = jnp.mean(jnp.square(g - mu), axis=-1, keepdims=True)
    vn = (g - mu) * lax.rsqrt(var + LN_EPS) * lng_ref[...] + lnb_ref[...]
    vn_ref[...] = vn
    vnb = vn.astype(BF16)
    row = lax.broadcasted_iota(jnp.int32, (CHUNK, CHUNK), 0)
    col = lax.broadcasted_iota(jnp.int32, (CHUNK, CHUNK), 1)
    causal = col <= row
    for gi in range(A_GROUPS):
        wsg = jnp.where(causal, ws_ref[0, gi], jnp.zeros((), BF16))
        cols = slice(gi * A_GROUP_DIM, (gi + 1) * A_GROUP_DIM)
        for c in range(h.shape[0] // CHUNK):
            rows = slice(c * CHUNK, (c + 1) * CHUNK)
            s = _dot(wsg, vnb[rows, cols]) + bs_ref[0, gi]
            oa_ref[rows, cols] = (u[rows, cols] * s).astype(oa_ref.dtype)


def _proj_a(x, w, lng, lnb, ws2, bs2, n_prompt, *, tm=512):
    m, k = x.shape
    n = w.shape[1]
    tm = _tile(math.gcd(m, n_prompt), tm)
    npt = n_prompt // tm
    sel = lambda i: (jnp.where(i >= npt, 1, 0), 0, 0, 0)
    return pl.pallas_call(
        _proj_a_kernel,
        out_shape=(jax.ShapeDtypeStruct((m, n // 2), BF16), jax.ShapeDtypeStruct((m, n // 2), F32)),
        grid=(m // tm,),
        in_specs=[pl.BlockSpec((tm, k), lambda i: (i, 0)),
                  pl.BlockSpec((k, n), lambda i: (0, 0)),
                  pl.BlockSpec((1, n // 2), lambda i: (0, 0)),
                  pl.BlockSpec((1, n // 2), lambda i: (0, 0)),
                  pl.BlockSpec((1, A_GROUPS, CHUNK, CHUNK), sel),
                  pl.BlockSpec((1, A_GROUPS, CHUNK, CHUNK), sel)],
        out_specs=(pl.BlockSpec((tm, n // 2), lambda i: (i, 0)),
                   pl.BlockSpec((tm, n // 2), lambda i: (i, 0))),
        compiler_params=_params("parallel"),
        name="proj_a",
    )(x, w, lng, lnb, ws2, bs2)


def _diff_lambda(lam_ref, lam_init):
    lv = lam_ref[...]
    s1 = jnp.sum(lv[0:1] * lv[1:2], axis=-1, keepdims=True)
    s2 = jnp.sum(lv[2:3] * lv[3:4], axis=-1, keepdims=True)
    return jnp.exp(s1) - jnp.exp(s2) + lam_init


def _stack_maps(q):
    lane = lax.broadcasted_iota(jnp.int32, q.shape, 1)
    zero = jnp.zeros((), q.dtype)
    return jnp.concatenate([jnp.where(lane < B_HEAD_DIM, q, zero),
                            jnp.where(lane >= B_HEAD_DIM, q, zero)], axis=0)


def _sub_norm(o, g, lam_init):
    r = lax.rsqrt(jnp.mean(o * o, axis=-1, keepdims=True) + LN_EPS)
    return o * r * g * (1.0 - lam_init)


def _attn_p_kernel(lam_ref, q_ref, kt_ref, v_ref, g_ref, o_ref, kt_scr, vb_scr, *, tq, nh, lam_init):
    qi = pl.program_id(2)
    hd = B_V_DIM

    @pl.when(qi == 0)
    def _():
        for kb in range(kt_scr.shape[0]):
            kt_scr[kb] = kt_ref[0, :, kb * tq:(kb + 1) * tq].astype(BF16)
        vb_scr[...] = v_ref[...].astype(BF16)

    qs = [_stack_maps(q_ref[:, h * hd:(h + 1) * hd]) for h in range(nh)]

    def step(carry, kb, diag):
        off = pl.multiple_of(kb * tq, tq)
        kt = kt_scr[kb]
        out = []
        for h in range(nh):
            s = _dot(qs[h], kt[h * hd:(h + 1) * hd])
            if diag:
                row = lax.broadcasted_iota(jnp.int32, s.shape, 0) % tq
                col = lax.broadcasted_iota(jnp.int32, s.shape, 1)
                s = jnp.where(col <= row, s, -jnp.inf)
            m, l, acc = carry[h]
            m_new = jnp.maximum(m, jnp.max(s, axis=-1, keepdims=True))
            a = jnp.exp(m - m_new)
            p = jnp.exp(s - m_new)
            l = a * l + jnp.sum(p, axis=-1, keepdims=True)
            acc = a * acc + _dot(p.astype(BF16), vb_scr[pl.ds(off, tq), h * hd:(h + 1) * hd])
            out.append((m_new, l, acc))
        return tuple(out)

    init = tuple((jnp.full((2 * tq, 1), -jnp.inf, F32), jnp.zeros((2 * tq, 1), F32),
                  jnp.zeros((2 * tq, hd), F32)) for _ in range(nh))
    carry = lax.fori_loop(0, qi, lambda kb, c: step(c, kb, False), init)
    carry = step(carry, qi, True)
    lam = _diff_lambda(lam_ref, lam_init)
    for h in range(nh):
        _, l, acc = carry[h]
        o = acc / l
        o = o[:tq] - lam * o[tq:]
        o_ref[:, h * hd:(h + 1) * hd] = _sub_norm(o, g_ref[...], lam_init).astype(o_ref.dtype)


def _attn_prompt(lamv, q, kt, v, g, batch, seq, n_heads, lam_init, *, tq=1024, nh=2):
    tq = _tile(seq, tq)
    nq = seq // tq
    nh = math.gcd(nh, n_heads)
    w = nh * B_V_DIM
    return pl.pallas_call(
        functools.partial(_attn_p_kernel, tq=tq, nh=nh, lam_init=lam_init),
        out_shape=jax.ShapeDtypeStruct((batch * seq, n_heads * B_V_DIM), BF16),
        grid=(batch, n_heads // nh, nq),
        in_specs=[pl.BlockSpec(lamv.shape, lambda b, h, i: (0, 0)),
                  pl.BlockSpec((tq, w), lambda b, h, i: (b * nq + i, h)),
                  pl.BlockSpec((1, w, seq), lambda b, h, i: (b, h, 0)),
                  pl.BlockSpec((seq, w), lambda b, h, i: (b, h)),
                  pl.BlockSpec((1, B_V_DIM), lambda b, h, i: (0, 0))],
        out_specs=pl.BlockSpec((tq, w), lambda b, h, i: (b * nq + i, h)),
        scratch_shapes=[pltpu.VMEM((nq, w, tq), BF16), pltpu.VMEM((seq, w), BF16)],
        compiler_params=_params("parallel", "parallel", "arbitrary"),
        name="attn_prompt",
    )(lamv, q, kt, v, g)


def _attn_s_kernel(pt_ref, lam_ref, q_ref, kn_ref, vn_ref, g_ref, *refs, n_pages, n_heads, page, dec,
                   lam_init):
    kp = refs[:n_pages]
    vp = refs[n_pages:2 * n_pages]
    o_ref = refs[2 * n_pages]
    s_scr, kn_scr, vn_scr = refs[2 * n_pages + 1:]

    @pl.when(pl.program_id(0) == 0)
    def _():
        kn_scr[...] = jnp.zeros_like(kn_scr)
        vn_scr[...] = jnp.zeros_like(vn_scr)

    kn_scr[0:dec, :] = kn_ref[0]
    vn_scr[0:dec, :] = vn_ref[0]
    lam = _diff_lambda(lam_ref, lam_init)
    past = n_pages * page
    for h in range(n_heads):
        cols = slice(h * B_V_DIM, (h + 1) * B_V_DIM)
        qs = _stack_maps(q_ref[0, :, cols])
        for p in range(n_pages):
            s_scr[:, p * page:(p + 1) * page] = _dot(qs, kp[p][0, h].astype(BF16))
        sn = _dot_t(qs, kn_scr[:, cols].astype(BF16))
        row = lax.broadcasted_iota(jnp.int32, sn.shape, 0) % dec
        col = lax.broadcasted_iota(jnp.int32, sn.shape, 1)
        s_scr[:, past:] = jnp.where(col <= row, sn, -jnp.inf)
        s = s_scr[...]
        m = jnp.max(s, axis=-1, keepdims=True)
        e = jnp.exp(s - m)
        pr = (e / jnp.sum(e, axis=-1, keepdims=True)).astype(BF16)
        o = _dot(pr[:, past:], vn_scr[:, cols].astype(BF16))
        for p in range(n_pages):
            vh = vp[p][0, pl.ds(h, page, stride=n_heads), :]
            o = o + _dot(pr[:, p * page:(p + 1) * page], vh.astype(BF16))
        o = o[:dec] - lam * o[dec:]
        o_ref[0, :, cols] = _sub_norm(o, g_ref[...], lam_init).astype(o_ref.dtype)


def _attn_sample(page_table, lamv, q, k_new, v_new, g, cache_kt, cache_v, lam_init):
    n_seq, dec, width = q.shape
    n_pages = page_table.shape[1]
    n_heads = width // B_V_DIM
    page = cache_kt.shape[3]
    tok = lambda b, pt: (b, 0, 0)
    const = lambda b, pt: (0, 0)

    def k_spec(p):
        return pl.BlockSpec((1, n_heads, B_V_DIM, page), lambda b, pt: (pt[b, p], 0, 0, 0))

    def v_spec(p):
        return pl.BlockSpec((1, page * n_heads, B_V_DIM), lambda b, pt: (pt[b, p], 0, 0))

    grid_spec = pltpu.PrefetchScalarGridSpec(
        num_scalar_prefetch=1,
        grid=(n_seq,),
        in_specs=[pl.BlockSpec(lamv.shape, const),
                  pl.BlockSpec((1, dec, width), tok),
                  pl.BlockSpec((1, dec, width), tok),
                  pl.BlockSpec((1, dec, width), tok),
                  pl.BlockSpec((1, B_V_DIM), const)]
                 + [k_spec(p) for p in range(n_pages)]
                 + [v_spec(p) for p in range(n_pages)],
        out_specs=pl.BlockSpec((1, dec, width), tok),
        scratch_shapes=[pltpu.VMEM((2 * dec, (n_pages + 1) * page), F32),
                        pltpu.VMEM((page, width), F32),
                        pltpu.VMEM((page, width), F32)],
    )
    return pl.pallas_call(
        functools.partial(_attn_s_kernel, n_pages=n_pages, n_heads=n_heads, page=page, dec=dec,
                          lam_init=lam_init),
        out_shape=jax.ShapeDtypeStruct((n_seq, dec, width), BF16),
        grid_spec=grid_spec,
        compiler_params=_params("arbitrary"),
        name="attn_sample",
    )(page_table, lamv, q, k_new, v_new, g, *([cache_kt] * n_pages), *([cache_v] * n_pages))


def _mem_attn_kernel(q_ref, k_ref, v_ref, o_ref, *, n_heads, interleaved):
    tq = q_ref.shape[1]
    pad = (-tq) % 16
    n_mem = k_ref.shape[1] // n_heads if interleaved else k_ref.shape[1]
    for b in range(q_ref.shape[0]):
        for h in range(n_heads):
            cols = slice(h * C_HEAD_DIM, (h + 1) * C_HEAD_DIM)
            if interleaved:
                kh = k_ref[b, pl.ds(h, n_mem, stride=n_heads), :]
                vh = v_ref[b, pl.ds(h, n_mem, stride=n_heads), :]
            else:
                kh, vh = k_ref[b, :, cols], v_ref[b, :, cols]
            q = q_ref[b, :, cols]
            if pad:
                q = jnp.concatenate([q, jnp.zeros((pad, C_HEAD_DIM), q.dtype)], axis=0)
            s = _dot_t(q, kh.astype(BF16)) * (C_HEAD_DIM ** -0.5)
            m = jnp.max(s, axis=-1, keepdims=True)
            e = jnp.exp(s - m)
            pr = (e / jnp.sum(e, axis=-1, keepdims=True)).astype(BF16)
            o = _dot(pr, vh.astype(BF16))
            o_ref[b, :, cols] = o[:tq].astype(o_ref.dtype)


def _mem_attn(q, mk, mv, *, interleaved, tq=512, bs=1):
    b, t, w = q.shape
    tq = _tile(t, tq)
    bs = math.gcd(bs, b)
    kv_spec = pl.BlockSpec((bs,) + mk.shape[1:], lambda i, j: (i, 0, 0))
    return pl.pallas_call(
        functools.partial(_mem_attn_kernel, n_heads=w // C_HEAD_DIM, interleaved=interleaved),
        out_shape=jax.ShapeDtypeStruct((b, t, w), BF16),
        grid=(b // bs, t // tq),
        in_specs=[pl.BlockSpec((bs, tq, w), lambda i, j: (i, j, 0)), kv_spec, kv_spec],
        out_specs=pl.BlockSpec((bs, tq, w), lambda i, j: (i, j, 0)),
        compiler_params=_params("parallel", "arbitrary"),
        name="mem_attn",
    )(q, mk, mv)


def _merge_kernel(oa_ref, obp_ref, obs_ref, ocp_ref, ocs_ref, ga_ref, gb_ref, gc_ref, wa_ref, wb_ref, wc_ref, m_ref,
                  *, npt):
    m = ga_ref[...].astype(F32) * _dot(oa_ref[...], wa_ref[...])
    m = m + gb_ref[...].astype(F32) * _dot(_group_rows(obp_ref, obs_ref, npt), wb_ref[...])
    m = m + gc_ref[...].astype(F32) * _dot(_group_rows(ocp_ref, ocs_ref, npt), wc_ref[...])
    m_ref[...] = m.astype(m_ref.dtype)


def _merge(oa, ob_p, ob_s, oc_p, oc_s, gates, wa, wb, wc, *, tm=1024, tn=512):
    t = oa.shape[0]
    d = wa.shape[1]
    tp = ob_p.shape[0]
    tm, tn = _tile(math.gcd(tp, ob_s.shape[0]), tm), _tile(d, tn)
    npt = tp // tm
    nj = d // tn
    gate = lambda b: pl.BlockSpec((tm, tn), lambda i, j: (i, b * nj + j))
    wcol = lambda width: pl.BlockSpec((width, tn), lambda i, j: (0, j))
    return pl.pallas_call(
        functools.partial(_merge_kernel, npt=npt),
        out_shape=jax.ShapeDtypeStruct((t, d), BF16),
        grid=(t // tm, nj),
        in_specs=[pl.BlockSpec((tm, oa.shape[1]), lambda i, j: (i, 0)),
                  *_group_specs(tm, ob_p.shape[1], npt), *_group_specs(tm, oc_p.shape[1], npt),
                  gate(0), gate(1), gate(2), wcol(wa.shape[0]), wcol(wb.shape[0]), wcol(wc.shape[0])],
        out_specs=pl.BlockSpec((tm, tn), lambda i, j: (i, j)),
        compiler_params=_params("parallel", "arbitrary"),
        name="merge",
    )(oa, ob_p, ob_s, oc_p, oc_s, gates, gates, gates, wa, wb, wc)


def _layer_norm(x, g, b):
    mu = jnp.mean(x, axis=-1, keepdims=True)
    var = jnp.mean(jnp.square(x - mu), axis=-1, keepdims=True)
    return (x - mu) * lax.rsqrt(var + LN_EPS) * g + b


def _pack_pairs(x):
    c = x.shape[1] // 2
    bits = lax.bitcast_convert_type(x.astype(BF16).astype(F32), jnp.uint32)
    return bits[:, c:] | (bits[:, :c] >> 16)


def _unpack_pairs(w):
    lo = lax.bitcast_convert_type(w << 16, F32)
    hi = lax.bitcast_convert_type(w & jnp.uint32(0xFFFF0000), F32)
    return jnp.concatenate([lo, hi], axis=1)


def _group_specs(tm, width, npt):
    return (pl.BlockSpec((tm, width), lambda i, *_: (jnp.minimum(i, npt - 1), 0)),
            pl.BlockSpec((tm, width), lambda i, *_: (jnp.maximum(i - npt, 0), 0)))


def _group_rows(p_ref, s_ref, npt):
    return jnp.where(pl.program_id(0) < npt, p_ref[...], s_ref[...])


def _out_kernel(m_ref, w_ref, xp_ref, xs_ref, g_ref, b_ref, h_ref, hb_ref, hp_ref, *, alpha, npt):
    y = alpha * _group_rows(xp_ref, xs_ref, npt) + _dot(m_ref[...], w_ref[...])
    h = _layer_norm(y, g_ref[...], b_ref[...])
    h_ref[...] = h
    hb_ref[...] = h.astype(BF16)
    hp_ref[...] = _pack_pairs(h)


def _out_proj(m, w_o, x_p, x_s, g, b, alpha, *, tm=512):
    t, d = m.shape
    tp = x_p.shape[0]
    tm = _tile(math.gcd(tp, x_s.shape[0]), tm)
    npt = tp // tm
    row = pl.BlockSpec((tm, d), lambda i: (i, 0))
    half = pl.BlockSpec((tm, d // 2), lambda i: (i, 0))
    vec = pl.BlockSpec((1, d), lambda i: (0, 0))
    return pl.pallas_call(
        functools.partial(_out_kernel, alpha=alpha, npt=npt),
        out_shape=(jax.ShapeDtypeStruct((t, d), F32), jax.ShapeDtypeStruct((t, d), BF16),
                   jax.ShapeDtypeStruct((t, d // 2), jnp.uint32)),
        grid=(t // tm,),
        in_specs=[row, pl.BlockSpec((d, d), lambda i: (0, 0)), *_group_specs(tm, d, npt), vec, vec],
        out_specs=(row, row, half),
        compiler_params=_params("parallel"),
        name="out_proj_ln1",
    )(m, w_o, x_p, x_s, g, b)


def _first_argmax(vals, idx, sentinel):
    m = jnp.max(vals, axis=0, keepdims=True)
    first = jnp.min(jnp.where(vals == m, idx, sentinel), axis=0, keepdims=True)
    return m, first


def _router_kernel(x_ref, w_ref, bias_ref, idx_ref, wt_ref, rank_ref, cnt_ref, run_scr):
    @pl.when(pl.program_id(0) == 0)
    def _():
        run_scr[...] = jnp.zeros_like(run_scr)

    s = jax.nn.sigmoid(_dot(x_ref[...], w_ref[...]))
    st = s.T
    sb = st + bias_ref[...]
    n_exp, tm = st.shape
    gs = n_exp // N_GROUPS
    neg = -jnp.inf
    ridx = lax.broadcasted_iota(jnp.int32, (gs, tm), 0)
    grp = []
    for g in range(N_GROUPS):
        blk = sb[g * gs:(g + 1) * gs]
        m1, first = _first_argmax(blk, ridx, gs)
        m2 = jnp.max(jnp.where(ridx == first, neg, blk), axis=0, keepdims=True)
        grp.append(m1 + m2)
    work = jnp.concatenate(grp, axis=0)
    gidx = lax.broadcasted_iota(jnp.int32, work.shape, 0)
    gmask = jnp.zeros(work.shape, jnp.bool_)
    for _ in range(TOPK_GROUPS):
        _, first = _first_argmax(work, gidx, N_GROUPS)
        sel = gidx == first
        gmask = jnp.logical_or(gmask, sel)
        work = jnp.where(sel, neg, work)
    masked = jnp.concatenate(
        [jnp.where(gmask[g:g + 1], sb[g * gs:(g + 1) * gs], neg) for g in range(N_GROUPS)], axis=0)
    eidx = lax.broadcasted_iota(jnp.int32, masked.shape, 0)
    ids, wts, sels = [], [], []
    for _ in range(TOP_K):
        _, first = _first_argmax(masked, eidx, n_exp)
        sel = eidx == first
        ids.append(first)
        sels.append(sel)
        wts.append(jnp.sum(jnp.where(sel, st, 0.0), axis=0, keepdims=True))
        masked = jnp.where(sel, neg, masked)
    wt = jnp.concatenate(wts, axis=0)
    idx_ref[...] = jnp.concatenate(ids, axis=0)
    wt_ref[...] = wt / jnp.sum(wt, axis=0, keepdims=True) * ROUTED_SCALE
    chosen = sels[0].astype(F32)
    for sel in sels[1:]:
        chosen = chosen + sel.astype(F32)
    earlier = lax.broadcasted_iota(jnp.int32, (tm, tm), 0) < lax.broadcasted_iota(jnp.int32, (tm, tm), 1)
    before = _dot(chosen.astype(BF16), earlier.astype(BF16)) + run_scr[...]
    rank_ref[...] = jnp.concatenate(
        [jnp.sum(jnp.where(sel, before, 0.0), axis=0, keepdims=True) for sel in sels], axis=0).astype(jnp.int32)
    run_scr[...] = run_scr[...] + jnp.sum(chosen, axis=1, keepdims=True)
    cnt_ref[...] = run_scr[...].astype(jnp.int32)


def _router(hb, w_router, bias, *, tm=256):
    t, d = hb.shape
    n_exp = w_router.shape[1]
    tm = _tile(t, tm)
    slot = pl.BlockSpec((TOP_K, tm), lambda i: (0, i))
    return pl.pallas_call(
        _router_kernel,
        out_shape=(jax.ShapeDtypeStruct((TOP_K, t), jnp.int32), jax.ShapeDtypeStruct((TOP_K, t), F32),
                   jax.ShapeDtypeStruct((TOP_K, t), jnp.int32), jax.ShapeDtypeStruct((n_exp, 1), jnp.int32)),
        grid=(t // tm,),
        in_specs=[pl.BlockSpec((tm, d), lambda i: (i, 0)),
                  pl.BlockSpec((d, n_exp), lambda i: (0, 0)),
                  pl.BlockSpec((n_exp, 1), lambda i: (0, 0))],
        out_specs=(slot, slot, slot, pl.BlockSpec((n_exp, 1), lambda i: (0, 0))),
        scratch_shapes=[pltpu.VMEM((n_exp, 1), F32)],
        compiler_params=_params("arbitrary"),
        name="router",
    )(hb, w_router, bias)


def _expert_kernel(b0_ref, nb_ref, nu_ref, x_hbm, wg_hbm, wu_hbm, wd_hbm, y_hbm, xbuf, ybuf, wbuf, sem_in, sem_out,
                   sem_w, w1_scr, wd_scr, *, f, up_off, n_blocks):
    e = pl.program_id(0)
    last = pl.num_programs(0) - 1
    b0 = b0_ref[e]
    nb = nb_ref[e]
    n_used = nu_ref[0]

    def x_copy(blk):
        slot = blk % X_SLOTS
        return pltpu.make_async_copy(x_hbm.at[pl.ds(blk * MOE_ROWS, MOE_ROWS)], xbuf.at[slot], sem_in.at[slot])

    def y_copy(blk):
        slot = blk % 2
        return pltpu.make_async_copy(ybuf.at[slot], y_hbm.at[pl.ds(blk * MOE_ROWS, MOE_ROWS)], sem_out.at[slot])

    def w_copies(ex):
        slot = ex % 2
        return [pltpu.make_async_copy(src.at[ex], wbuf.at[slot, j], sem_w.at[slot, j])
                for j, src in enumerate((wg_hbm, wu_hbm, wd_hbm))]

    @pl.when(e == 0)
    def _():
        for cp in w_copies(0):
            cp.start(priority=1)
        w1_scr[f:up_off, :] = jnp.zeros((up_off - f, w1_scr.shape[1]), BF16)

        for blk in range(2):
            @pl.when(blk < n_used)
            def _():
                x_copy(blk).start()

    for cp in w_copies(e):
        cp.wait()

    @pl.when(e < last)
    def _():
        for cp in w_copies(e + 1):
            cp.start(priority=1)

    def process(blocks):
        for g in blocks:
            x_copy(g).wait()
        for g in blocks:
            @pl.when(g + 2 < n_used)
            def _():
                x_copy(g + 2).start()

            @pl.when(g >= 2)
            def _():
                y_copy(g - 2).wait()

        hs = [_dot_t(_unpack_pairs(xbuf[g % X_SLOTS]).astype(BF16), w1_scr[...]) for g in blocks]
        acts = [jax.nn.silu(h[:, :f]) * h[:, up_off:up_off + f] for h in hs]
        for g, a in zip(blocks, acts):
            ybuf[g % 2] = _pack_pairs(_dot(a.astype(BF16), wd_scr[...]))
            y_copy(g).start()

    @pl.when(nb > 0)
    def _():
        wslot = e % 2
        w1_scr[0:f, :] = wbuf[wslot, 0].astype(BF16)
        w1_scr[up_off:up_off + f, :] = wbuf[wslot, 1].astype(BF16)
        wd_scr[...] = wbuf[wslot, 2].astype(BF16)

        def pair(c, carry):
            g = b0 + 2 * c
            process([g, g + 1])
            return carry

        lax.fori_loop(0, nb // 2, pair, 0)

        @pl.when(nb % 2 == 1)
        def _():
            process([b0 + nb - 1])

    @pl.when(e == last)
    def _():
        @pl.when(n_used >= 2)
        def _():
            y_copy(n_used - 2).wait()

        @pl.when(n_used >= 1)
        def _():
            y_copy(n_used - 1).wait()

        ybuf[0] = jnp.zeros(ybuf.shape[1:], ybuf.dtype)

        def fill(blk, carry):
            cp = pltpu.make_async_copy(ybuf.at[0], y_hbm.at[pl.ds(blk * MOE_ROWS, MOE_ROWS)], sem_out.at[0])
            cp.start()
            cp.wait()
            return carry

        lax.fori_loop(n_used, n_blocks, fill, 0)


def _experts(blk0, nblk, n_used, x_sorted, w_gate_t, w_up_t, w_down):
    n_rows = x_sorted.shape[0]
    n_exp, f, d = w_down.shape
    dp = d // 2
    up_off = -(-f // LANES) * LANES
    n_blocks = n_rows // MOE_ROWS
    hbm = pl.BlockSpec(memory_space=pl.ANY)
    grid_spec = pltpu.PrefetchScalarGridSpec(
        num_scalar_prefetch=3,
        grid=(n_exp,),
        in_specs=[hbm, hbm, hbm, hbm],
        out_specs=hbm,
        scratch_shapes=[pltpu.VMEM((X_SLOTS, MOE_ROWS, dp), jnp.uint32), pltpu.VMEM((2, MOE_ROWS, dp), jnp.uint32),
                        pltpu.VMEM((2, 3, f, d), F32),
                        pltpu.SemaphoreType.DMA((X_SLOTS,)), pltpu.SemaphoreType.DMA((2,)),
                        pltpu.SemaphoreType.DMA((2, 3)),
                        pltpu.VMEM((up_off + f, d), BF16), pltpu.VMEM((f, d), BF16)],
    )
    return pl.pallas_call(
        functools.partial(_expert_kernel, f=f, up_off=up_off, n_blocks=n_blocks),
        out_shape=jax.ShapeDtypeStruct((n_rows, dp), jnp.uint32),
        grid_spec=grid_spec,
        compiler_params=_params("arbitrary"),
        name="experts",
    )(blk0, nblk, n_used, x_sorted, w_gate_t, w_up_t, w_down)


def _final_kernel(h_ref, hb_ref, yg_ref, wt_ref, wg_ref, wu_ref, wd_ref, g_ref, b_ref, op_ref, os_ref, *, alpha,
                  npt):
    hb = hb_ref[...]
    a = jax.nn.silu(_dot_t(hb, wg_ref[...])) * _dot_t(hb, wu_ref[...])
    shared = _dot(a.astype(BF16), wd_ref[...])
    wt = wt_ref[...]
    routed = wt[:, 0:1] * _unpack_pairs(yg_ref[0])
    for k in range(1, yg_ref.shape[0]):
        routed = routed + wt[:, k:k + 1] * _unpack_pairs(yg_ref[k])
    y = _layer_norm(alpha * h_ref[...] + (routed + shared), g_ref[...], b_ref[...])

    @pl.when(pl.program_id(0) < npt)
    def _():
        op_ref[...] = y

    @pl.when(pl.program_id(0) >= npt)
    def _():
        os_ref[...] = y


def _final(h, hb, yg, wt, ws_gate_t, ws_up_t, ws_down, g, b, alpha, tp, *, tm=256):
    t, d = h.shape
    f = ws_down.shape[0]
    tm = _tile(math.gcd(tp, t - tp), tm)
    npt = tp // tm
    row = pl.BlockSpec((tm, d), lambda i: (i, 0))
    vec = pl.BlockSpec((1, d), lambda i: (0, 0))
    wfull = pl.BlockSpec((f, d), lambda i: (0, 0))
    return pl.pallas_call(
        functools.partial(_final_kernel, alpha=alpha, npt=npt),
        out_shape=(jax.ShapeDtypeStruct((tp, d), F32), jax.ShapeDtypeStruct((t - tp, d), F32)),
        grid=(t // tm,),
        in_specs=[row, row, pl.BlockSpec((yg.shape[0], tm, d // 2), lambda i: (0, i, 0)),
                  pl.BlockSpec((tm, wt.shape[1]), lambda i: (i, 0)), wfull, wfull, wfull, vec, vec],
        out_specs=_group_specs(tm, d, npt),
        compiler_params=_params("arbitrary"),
        name="shared_combine_ln2",
    )(h, hb, yg, wt, ws_gate_t, ws_up_t, ws_down, g, b)


def _dispatch(idx_t, rank_t, counts):
    top_k, t = idx_t.shape
    n_exp = counts.shape[0]
    tk = t * top_k
    pcounts = (counts + MOE_ROWS - 1) // MOE_ROWS * MOE_ROWS
    pend = jnp.cumsum(pcounts)
    pstart = pend - pcounts
    onehot = idx_t[:, :, None] == jnp.arange(n_exp, dtype=jnp.int32)
    dest = rank_t + jnp.sum(jnp.where(onehot, pstart, 0), axis=-1)
    n_blocks = -(-tk // MOE_ROWS) + n_exp
    n_used = (pend[-1:] // MOE_ROWS).astype(jnp.int32)
    n_rows = n_blocks * MOE_ROWS
    assert n_rows == tk + n_exp * MOE_ROWS
    i = jnp.arange(MOE_ROWS, dtype=jnp.int32)[None, :]
    pad_keys = jnp.where(i < (pcounts - counts)[:, None], (pstart + counts)[:, None] + i, n_rows)
    keys = jnp.concatenate([dest.reshape(-1), pad_keys.reshape(-1)]).astype(jnp.int32)
    toks = jnp.concatenate([jnp.tile(jnp.arange(t, dtype=jnp.int32), top_k),
                            jnp.arange(n_exp * MOE_ROWS, dtype=jnp.int32) % t])
    _, row_tok = lax.sort((keys, toks), num_keys=1)
    return dest, row_tok, (pstart // MOE_ROWS).astype(jnp.int32), (pcounts // MOE_ROWS).astype(jnp.int32), n_used


def _moe(h, hb, hp, w_router, router_bias, w_gate_t, w_up_t, w_down, ws_gate_t, ws_up_t, ws_down, g, b, alpha, tp):
    n_exp = w_router.shape[1]
    idx_t, wt_t, rank_t, counts = _router(hb, w_router, router_bias.reshape(n_exp, 1))
    dest, row_tok, blk0, nblk, n_used = _dispatch(idx_t, rank_t, counts.reshape(n_exp))
    x_sorted = hp[row_tok]
    y_sorted = _experts(blk0, nblk, n_used, x_sorted, w_gate_t, w_up_t, w_down)
    yg = y_sorted[dest]
    return _final(h, hb, yg, wt_t.T, ws_gate_t, ws_up_t, ws_down, g, b, alpha, tp)


def _rope_tables(pos):
    half = B_HEAD_DIM // 2
    inv = 1.0 / (ROPE_THETA ** (jnp.arange(half, dtype=F32) / half))
    ang = pos.astype(F32)[:, None] * inv[None, :]
    cos, sin = jnp.cos(ang), jnp.sin(ang)
    reps = LANES // B_HEAD_DIM
    return (jnp.concatenate([cos, cos] * reps, axis=1), jnp.concatenate([-sin, sin] * reps, axis=1))


def kernel(x_prompt, x_sample, mem_prompt, cache_k, cache_v, page_table, cache_mem_k, cache_mem_v, w_in, a_ln_g, a_ln_b, a_ws, a_bs, lam_q1, lam_k1, lam_q2, lam_k2, b_subln_g, w_mk, w_mv, w_pa, w_pb, w_pc, w_o, ln1_g, ln1_b, w_router, router_bias, w_gate, w_up, w_down, ws_gate, ws_up, ws_down, ln2_g, ln2_b):
    depth = w_in.shape[0]
    assert depth == 1, "single-layer step"
    layer = 0
    batch, seq, d = x_prompt.shape
    n_seq, dec, _ = x_sample.shape
    n_mem = mem_prompt.shape[1]
    n_phys, page, n_bh = cache_k.shape[1:4]
    past = page_table.shape[1] * page
    c_width = cache_mem_k.shape[3] * C_HEAD_DIM
    a_width = A_GROUPS * A_GROUP_DIM
    qk_width = n_bh * 2 * B_HEAD_DIM
    bv_width = n_bh * B_V_DIM
    assert CHUNK % dec == 0 and seq % CHUNK == 0
    alpha = (2.0 * depth) ** 0.25
    lam_init = _lambda_init(layer)
    tp, ts = batch * seq, n_seq * dec
    t = tp + ts

    x_p, x_s = x_prompt.reshape(tp, d), x_sample.reshape(ts, d)
    xb = jnp.concatenate([x_p.astype(BF16), x_s.astype(BF16)], axis=0)
    w_in_b = w_in[layer].astype(BF16)
    o = [0, 2 * a_width]
    for wdt in (qk_width, qk_width, bv_width, c_width, 3 * d):
        o.append(o[-1] + wdt)
    pos = jnp.concatenate([jnp.tile(jnp.arange(seq), batch), past + jnp.tile(jnp.arange(dec), n_seq)])
    rope = _rope_tables(pos)

    reps = CHUNK // dec
    ws = a_ws[layer]
    ws_s = jnp.einsum("ab,gts->gatbs", jnp.eye(reps, dtype=ws.dtype), ws[:, :dec, :dec]).reshape(
        A_GROUPS, CHUNK, CHUNK)
    ws2 = jnp.stack([ws, ws_s]).astype(BF16)
    bs = a_bs[layer]
    bs2 = jnp.stack([bs, jnp.tile(bs[:, :dec], (1, reps))])
    bs2 = jnp.broadcast_to(bs2[..., None], bs2.shape + (CHUNK,)).astype(F32)

    o_a, vn = _proj_a(xb, w_in_b[:, o[0]:o[1]], a_ln_g[layer].reshape(1, a_width),
                      a_ln_b[layer].reshape(1, a_width), ws2, bs2, tp)
    q = _mm(xb, w_in_b[:, o[1]:o[2]], BF16, epilogue="rope", scale=B_HEAD_DIM ** -0.5, rope=rope)
    v_p = _mm(xb, w_in_b[:, o[3]:o[4]], F32, m=tp)
    v_s = _mm(xb[tp:], w_in_b[:, o[3]:o[4]], F32)
    cq = _mm(xb, w_in_b[:, o[4]:o[5]], BF16)
    gates = _mm(xb, w_in_b[:, o[5]:o[6]], BF16, epilogue="sigmoid")
    w_k = w_in_b[:, o[2]:o[3]]
    w_k_t = w_k.T
    cos_t, sin_t = rope[0][:, :B_HEAD_DIM // 2].T, rope[1][:, B_HEAD_DIM // 2:B_HEAD_DIM].T
    kt_p = _proj_kt(w_k_t, xb, cos_t, sin_t, batch, tp, tm=512)
    xs_t = jnp.swapaxes(x_sample, 0, 1).reshape(ts, d).astype(BF16)
    pos_s = past + jnp.repeat(jnp.arange(dec), n_seq)
    cos_s, sin_s = _rope_tables(pos_s)
    kt_s = _proj_kt(w_k_t, xs_t, cos_s[:, :B_HEAD_DIM // 2].T, sin_s[:, B_HEAD_DIM // 2:B_HEAD_DIM].T, dec,
                    ts, tm=n_seq)
    k_s = _mm(xb[tp:], w_k, F32, epilogue="rope", rope=(rope[0][tp:], rope[1][tp:]))

    lamv = jnp.stack([lam_q1[layer], lam_k1[layer], lam_q2[layer], lam_k2[layer]]).astype(F32)
    g_sub = b_subln_g[layer].reshape(1, B_V_DIM)
    ob_p = _attn_prompt(lamv, q, kt_p, v_p, g_sub, batch, seq, n_bh, lam_init)
    cache_kt = jnp.transpose(cache_k[layer], (0, 2, 3, 4, 1)).reshape(n_phys, n_bh, 2 * B_HEAD_DIM, page)
    ob_s = _attn_sample(page_table, lamv, q[tp:].reshape(n_seq, dec, qk_width),
                        k_s.reshape(n_seq, dec, qk_width), v_s.reshape(n_seq, dec, bv_width), g_sub,
                        cache_kt, cache_v[layer].reshape(n_phys, page * n_bh, B_V_DIM), lam_init)

    c_heads = c_width // C_HEAD_DIM
    memb = mem_prompt.reshape(batch * n_mem, d).astype(BF16)
    mk = _mm(memb, w_mk[layer].astype(BF16), F32)
    mv = _mm(memb, w_mv[layer].astype(BF16), F32)
    oc_p = _mem_attn(cq[:tp].reshape(batch, seq, c_width), mk.reshape(batch, n_mem, c_width),
                     mv.reshape(batch, n_mem, c_width), interleaved=False)
    oc_s = _mem_attn(cq[tp:].reshape(n_seq, dec, c_width),
                     cache_mem_k[layer].reshape(n_seq, n_mem * c_heads, C_HEAD_DIM),
                     cache_mem_v[layer].reshape(n_seq, n_mem * c_heads, C_HEAD_DIM), interleaved=True, bs=8)

    m = _merge(o_a, ob_p, ob_s.reshape(ts, bv_width), oc_p.reshape(tp, c_width), oc_s.reshape(ts, c_width), gates,
               w_pa[layer].astype(BF16), w_pb[layer].astype(BF16), w_pc[layer].astype(BF16))
    h1, h1b, h1p = _out_proj(m, w_o[layer].astype(BF16), x_p, x_s, ln1_g[layer].reshape(1, d),
                             ln1_b[layer].reshape(1, d), alpha)
    y_p, y_s = _moe(h1, h1b, h1p, w_router[layer].astype(BF16), router_bias[layer],
                    jnp.swapaxes(w_gate[layer], 1, 2), jnp.swapaxes(w_up[layer], 1, 2), w_down[layer],
                    ws_gate[layer].T.astype(BF16), ws_up[layer].T.astype(BF16), ws_down[layer].astype(BF16),
                    ln2_g[layer].reshape(1, d), ln2_b[layer].reshape(1, d), alpha, tp)

    k_prompt = jnp.transpose(kt_p.reshape(batch, n_bh, 2, B_HEAD_DIM, seq), (0, 4, 1, 2, 3))
    k_sample = jnp.transpose(kt_s.reshape(dec, n_bh, 2, B_HEAD_DIM, n_seq), (4, 0, 1, 2, 3))
    return (y_p.reshape(batch, seq, d),
            y_s.reshape(n_seq, dec, d),
            k_prompt[None],
            v_p.reshape(1, batch, seq, n_bh, B_V_DIM),
            mk.reshape(1, batch, n_mem, c_heads, C_HEAD_DIM),
            mv.reshape(1, batch, n_mem, c_heads, C_HEAD_DIM),
            k_sample[None],
            v_s.reshape(1, n_seq, dec, n_bh, B_V_DIM),
            vn[tp:].reshape(1, n_seq, dec, a_width))
```

```python
import functools
import math

import jax
import jax.numpy as jnp
from jax import lax
from jax.experimental import pallas as pl
from jax.experimental.pallas import tpu as pltpu

F32 = jnp.float32
BF16 = jnp.bfloat16

LN_EPS = 1e-5
ROPE_THETA = 10000.0
LANES = 128
A_GROUP_DIM = 128
A_GROUPS = 4
CHUNK = 128
B_HEAD_DIM = 64
B_V_DIM = 128
C_HEAD_DIM = 128
N_GROUPS = 8
TOPK_GROUPS = 4
TOP_K = 8
ROUTED_SCALE = 2.5
MOE_ROWS = 256
GROUP = 3
X_SLOTS = 2 * GROUP
Y_SLOTS = 4
VMEM_LIMIT = 56 * 1024 * 1024


def _lambda_init(layer):
    return 0.8 - 0.6 * math.exp(-0.3 * layer)


def _params(*sem):
    return pltpu.CompilerParams(dimension_semantics=sem, vmem_limit_bytes=VMEM_LIMIT)


def _tile(n, pref):
    t = min(n, pref)
    while n % t:
        t //= 2
    return t


def _dot(a, b):
    return jnp.dot(a, b, preferred_element_type=F32)


def _dot_t(a, b):
    return lax.dot_general(a, b, (((1,), (1,)), ((), ())), preferred_element_type=F32)


def _mm_kernel(*refs, epilogue, scale):
    if epilogue == "rope":
        x_ref, w_ref, cos_ref, sin_ref, o_ref = refs
    else:
        x_ref, w_ref, o_ref = refs
    acc = _dot(x_ref[...], w_ref[...])
    if epilogue == "rope":
        n = acc.shape[1] // LANES
        cos = jnp.concatenate([cos_ref[...]] * n, axis=1)
        sin = jnp.concatenate([sin_ref[...]] * n, axis=1)
        lane = lax.broadcasted_iota(jnp.int32, acc.shape, 1)
        first = (lane % B_HEAD_DIM) < (B_HEAD_DIM // 2)
        half = B_HEAD_DIM // 2
        swapped = jnp.where(first, pltpu.roll(acc, acc.shape[1] - half, 1), pltpu.roll(acc, half, 1))
        acc = acc * cos + swapped * sin
    elif epilogue == "sigmoid":
        acc = jax.nn.sigmoid(acc)
    if scale != 1.0:
        acc = acc * scale
    o_ref[...] = acc.astype(o_ref.dtype)


def _mm(x, w, out_dtype, *, epilogue="none", scale=1.0, rope=None, tn=1024, m=None):
    k = x.shape[1]
    m = x.shape[0] if m is None else m
    n = w.shape[1]
    tm, tn = _tile(m, 512 if epilogue == "rope" else 1024), _tile(n, tn)
    in_specs = [pl.BlockSpec((tm, k), lambda i, j: (i, 0)),
                pl.BlockSpec((k, tn), lambda i, j: (0, j))]
    args = [x, w]
    if epilogue == "rope":
        in_specs += [pl.BlockSpec((tm, LANES), lambda i, j: (i, 0))] * 2
        args += list(rope)
    return pl.pallas_call(
        functools.partial(_mm_kernel, epilogue=epilogue, scale=scale),
        out_shape=jax.ShapeDtypeStruct((m, n), out_dtype),
        grid=(m // tm, n // tn),
        in_specs=in_specs,
        out_specs=pl.BlockSpec((tm, tn), lambda i, j: (i, j)),
        compiler_params=_params("parallel", "arbitrary"),
        name="proj_" + epilogue,
    )(*args)


def _kt_kernel(w_ref, x_ref, cos_ref, sin_ref, o_ref):
    acc = _dot_t(w_ref[...], x_ref[...])
    cos, sin = cos_ref[...], sin_ref[...]
    half = B_HEAD_DIM // 2
    for g in range(acc.shape[0] // B_HEAD_DIM):
        lo = g * B_HEAD_DIM
        x1 = acc[lo:lo + half]
        x2 = acc[lo + half:lo + B_HEAD_DIM]
        o_ref[0, lo:lo + half, :] = x1 * cos - x2 * sin
        o_ref[0, lo + half:lo + B_HEAD_DIM, :] = x2 * cos + x1 * sin


def _proj_kt(w_t, x, cos_t, sin_t, n_groups, m, *, tm):
    n, k = w_t.shape
    per = m // n_groups
    tm = _tile(per, tm)
    nt = per // tm
    half = B_HEAD_DIM // 2
    return pl.pallas_call(
        _kt_kernel,
        out_shape=jax.ShapeDtypeStruct((n_groups, n, per), F32),
        grid=(m // tm,),
        in_specs=[pl.BlockSpec((n, k), lambda i: (0, 0)),
                  pl.BlockSpec((tm, k), lambda i: (i, 0)),
                  pl.BlockSpec((half, tm), lambda i: (0, i)),
                  pl.BlockSpec((half, tm), lambda i: (0, i))],
        out_specs=pl.BlockSpec((1, n, tm), lambda i: (i // nt, 0, i % nt)),
        compiler_params=_params("parallel"),
        name="proj_kt",
    )(w_t, x, cos_t, sin_t)


def _proj_a_kernel(x_ref, w_ref, lng_ref, lnb_ref, ws_ref, bs_ref, oa_ref, vn_ref):
    h = _dot(x_ref[...], w_ref[...])
    aw = h.shape[1] // 2
    u = jax.nn.gelu(h[:, :aw])
    g = jax.nn.gelu(h[:, aw:])
    mu = jnp.mean(g, axis=-1, keepdims=True)
    var = jnp.mean(jnp.square(g - mu), axis=-1, keepdims=True)
    vn = (g - mu) * lax.rsqrt(var + LN_EPS) * lng_ref[...] + lnb_ref[...]
    vn_ref[...] = vn
    vnb = vn.astype(BF16)
    row = lax.broadcasted_iota(jnp.int32, (CHUNK, CHUNK), 0)
    col = lax.broadcasted_iota(jnp.int32, (CHUNK, CHUNK), 1)
    causal = col <= row
    for gi in range(A_GROUPS):
        wsg = jnp.where(causal, ws_ref[0, gi], jnp.zeros((), BF16))
        cols = slice(gi * A_GROUP_DIM, (gi + 1) * A_GROUP_DIM)
        for c in range(h.shape[0] // CHUNK):
            rows = slice(c * CHUNK, (c + 1) * CHUNK)
            s = _dot(wsg, vnb[rows, cols]) + bs_ref[0, gi]
            oa_ref[rows, cols] = (u[rows, cols] * s).astype(oa_ref.dtype)


def _proj_a(x, w, lng, lnb, ws2, bs2, n_prompt, *, tm=512):
    m, k = x.shape
    n = w.shape[1]
    tm = _tile(math.gcd(m, n_prompt), tm)
    npt = n_prompt // tm
    sel = lambda i: (jnp.where(i >= npt, 1, 0), 0, 0, 0)
    return pl.pallas_call(
        _proj_a_kernel,
        out_shape=(jax.ShapeDtypeStruct((m, n // 2), BF16), jax.ShapeDtypeStruct((m, n // 2), F32)),
        grid=(m // tm,),
        in_specs=[pl.BlockSpec((tm, k), lambda i: (i, 0)),
                  pl.BlockSpec((k, n), lambda i: (0, 0)),
                  pl.BlockSpec((1, n // 2), lambda i: (0, 0)),
                  pl.BlockSpec((1, n // 2), lambda i: (0, 0)),
                  pl.BlockSpec((1, A_GROUPS, CHUNK, CHUNK), sel),
                  pl.BlockSpec((1, A_GROUPS, CHUNK, CHUNK), sel)],
        out_specs=(pl.BlockSpec((tm, n // 2), lambda i: (i, 0)),
                   pl.BlockSpec((tm, n // 2), lambda i: (i, 0))),
        compiler_params=_params("parallel"),
        name="proj_a",
    )(x, w, lng, lnb, ws2, bs2)


def _diff_lambda(lam_ref, lam_init):
    lv = lam_ref[...]
    s1 = jnp.sum(lv[0:1] * lv[1:2], axis=-1, keepdims=True)
    s2 = jnp.sum(lv[2:3] * lv[3:4], axis=-1, keepdims=True)
    return jnp.exp(s1) - jnp.exp(s2) + lam_init


def _stack_maps(q):
    lane = lax.broadcasted_iota(jnp.int32, q.shape, 1)
    zero = jnp.zeros((), q.dtype)
    return jnp.concatenate([jnp.where(lane < B_HEAD_DIM, q, zero),
                            jnp.where(lane >= B_HEAD_DIM, q, zero)], axis=0)


def _sub_norm(o, g, lam_init):
    r = lax.rsqrt(jnp.mean(o * o, axis=-1, keepdims=True) + LN_EPS)
    return o * r * g * (1.0 - lam_init)


def _attn_p_kernel(lam_ref, q_ref, kt_ref, v_ref, g_ref, o_ref, kt_scr, vb_scr, *, tq, nh, lam_init):
    qi = pl.program_id(2)
    hd = B_V_DIM

    @pl.when(qi == 0)
    def _():
        for kb in range(kt_scr.shape[0]):
            kt_scr[kb] = kt_ref[0, :, kb * tq:(kb + 1) * tq].astype(BF16)
        vb_scr[...] = v_ref[...].astype(BF16)

    qs = [_stack_maps(q_ref[:, h * hd:(h + 1) * hd]) for h in range(nh)]

    def step(carry, kb, diag):
        off = pl.multiple_of(kb * tq, tq)
        kt = kt_scr[kb]
        out = []
        for h in range(nh):
            s = _dot(qs[h], kt[h * hd:(h + 1) * hd])
            if diag:
                row = lax.broadcasted_iota(jnp.int32, s.shape, 0) % tq
                col = lax.broadcasted_iota(jnp.int32, s.shape, 1)
                s = jnp.where(col <= row, s, -jnp.inf)
            m, l, acc = carry[h]
            m_new = jnp.maximum(m, jnp.max(s, axis=-1, keepdims=True))
            a = jnp.exp(m - m_new)
            p = jnp.exp(s - m_new)
            l = a * l + jnp.sum(p, axis=-1, keepdims=True)
            acc = a * acc + _dot(p.astype(BF16), vb_scr[pl.ds(off, tq), h * hd:(h + 1) * hd])
            out.append((m_new, l, acc))
        return tuple(out)

    init = tuple((jnp.full((2 * tq, 1), -jnp.inf, F32), jnp.zeros((2 * tq, 1), F32),
                  jnp.zeros((2 * tq, hd), F32)) for _ in range(nh))
    carry = lax.fori_loop(0, qi, lambda kb, c: step(c, kb, False), init)
    carry = step(carry, qi, True)
    lam = _diff_lambda(lam_ref, lam_init)
    for h in range(nh):
        _, l, acc = carry[h]
        o = acc / l
        o = o[:tq] - lam * o[tq:]
        o_ref[:, h * hd:(h + 1) * hd] = _sub_norm(o, g_ref[...], lam_init).astype(o_ref.dtype)


def _attn_prompt(lamv, q, kt, v, g, batch, seq, n_heads, lam_init, *, tq=1024, nh=2):
    tq = _tile(seq, tq)
    nq = seq // tq
    nh = math.gcd(nh, n_heads)
    w = nh * B_V_DIM
    return pl.pallas_call(
        functools.partial(_attn_p_kernel, tq=tq, nh=nh, lam_init=lam_init),
        out_shape=jax.ShapeDtypeStruct((batch * seq, n_heads * B_V_DIM), BF16),
        grid=(batch, n_heads // nh, nq),
        in_specs=[pl.BlockSpec(lamv.shape, lambda b, h, i: (0, 0)),
                  pl.BlockSpec((tq, w), lambda b, h, i: (b * nq + i, h)),
                  pl.BlockSpec((1, w, seq), lambda b, h, i: (b, h, 0)),
                  pl.BlockSpec((seq, w), lambda b, h, i: (b, h)),
                  pl.BlockSpec((1, B_V_DIM), lambda b, h, i: (0, 0))],
        out_specs=pl.BlockSpec((tq, w), lambda b, h, i: (b * nq + i, h)),
        scratch_shapes=[pltpu.VMEM((nq, w, tq), BF16), pltpu.VMEM((seq, w), BF16)],
        compiler_params=_params("parallel", "parallel", "arbitrary"),
        name="attn_prompt",
    )(lamv, q, kt, v, g)


def _attn_s_kernel(pt_ref, lam_ref, q_ref, kn_ref, vn_ref, g_ref, *refs, n_pages, n_heads, page, dec,
                   lam_init):
    kp = refs[:n_pages]
    vp = refs[n_pages:2 * n_pages]
    o_ref = refs[2 * n_pages]
    s_scr, kn_scr, vn_scr = refs[2 * n_pages + 1:]

    @pl.when(pl.program_id(0) == 0)
    def _():
        kn_scr[...] = jnp.zeros_like(kn_scr)
        vn_scr[...] = jnp.zeros_like(vn_scr)

    kn_scr[0:dec, :] = kn_ref[0]
    vn_scr[0:dec, :] = vn_ref[0]
    lam = _diff_lambda(lam_ref, lam_init)
    past = n_pages * page
    for h in range(n_heads):
        cols = slice(h * B_V_DIM, (h + 1) * B_V_DIM)
        qs = _stack_maps(q_ref[0, :, cols])
        for p in range(n_pages):
            s_scr[:, p * page:(p + 1) * page] = _dot(qs, kp[p][0, h].astype(BF16))
        sn = _dot_t(qs, kn_scr[:, cols].astype(BF16))
        row = lax.broadcasted_iota(jnp.int32, sn.shape, 0) % dec
        col = lax.broadcasted_iota(jnp.int32, sn.shape, 1)
        s_scr[:, past:] = jnp.where(col <= row, sn, -jnp.inf)
        s = s_scr[...]
        m = jnp.max(s, axis=-1, keepdims=True)
        e = jnp.exp(s - m)
        pr = (e / jnp.sum(e, axis=-1, keepdims=True)).astype(BF16)
        o = _dot(pr[:, past:], vn_scr[:, cols].astype(BF16))
        for p in range(n_pages):
            vh = vp[p][0, pl.ds(h, page, stride=n_heads), :]
            o = o + _dot(pr[:, p * page:(p + 1) * page], vh.astype(BF16))
        o = o[:dec] - lam * o[dec:]
        o_ref[0, :, cols] = _sub_norm(o, g_ref[...], lam_init).astype(o_ref.dtype)


def _attn_sample(page_table, lamv, q, k_new, v_new, g, cache_kt, cache_v, lam_init):
    n_seq, dec, width = q.shape
    n_pages = page_table.shape[1]
    n_heads = width // B_V_DIM
    page = cache_kt.shape[3]
    tok = lambda b, pt: (b, 0, 0)
    const = lambda b, pt: (0, 0)

    def k_spec(p):
        return pl.BlockSpec((1, n_heads, B_V_DIM, page), lambda b, pt: (pt[b, p], 0, 0, 0))

    def v_spec(p):
        return pl.BlockSpec((1, page * n_heads, B_V_DIM), lambda b, pt: (pt[b, p], 0, 0))

    grid_spec = pltpu.PrefetchScalarGridSpec(
        num_scalar_prefetch=1,
        grid=(n_seq,),
        in_specs=[pl.BlockSpec(lamv.shape, const),
                  pl.BlockSpec((1, dec, width), tok),
                  pl.BlockSpec((1, dec, width), tok),
                  pl.BlockSpec((1, dec, width), tok),
                  pl.BlockSpec((1, B_V_DIM), const)]
                 + [k_spec(p) for p in range(n_pages)]
                 + [v_spec(p) for p in range(n_pages)],
        out_specs=pl.BlockSpec((1, dec, width), tok),
        scratch_shapes=[pltpu.VMEM((2 * dec, (n_pages + 1) * page), F32),
                        pltpu.VMEM((page, width), F32),
                        pltpu.VMEM((page, width), F32)],
    )
    return pl.pallas_call(
        functools.partial(_attn_s_kernel, n_pages=n_pages, n_heads=n_heads, page=page, dec=dec,
                          lam_init=lam_init),
        out_shape=jax.ShapeDtypeStruct((n_seq, dec, width), BF16),
        grid_spec=grid_spec,
        compiler_params=_params("arbitrary"),
        name="attn_sample",
    )(page_table, lamv, q, k_new, v_new, g, *([cache_kt] * n_pages), *([cache_v] * n_pages))


def _mem_attn_kernel(q_ref, k_ref, v_ref, o_ref, *, n_heads, interleaved):
    tq = q_ref.shape[1]
    pad = (-tq) % 16
    n_mem = k_ref.shape[1] // n_heads if interleaved else k_ref.shape[1]
    for b in range(q_ref.shape[0]):
        for h in range(n_heads):
            cols = slice(h * C_HEAD_DIM, (h + 1) * C_HEAD_DIM)
            if interleaved:
                kh = k_ref[b, pl.ds(h, n_mem, stride=n_heads), :]
                vh = v_ref[b, pl.ds(h, n_mem, stride=n_heads), :]
            else:
                kh, vh = k_ref[b, :, cols], v_ref[b, :, cols]
            q = q_ref[b, :, cols]
            if pad:
                q = jnp.concatenate([q, jnp.zeros((pad, C_HEAD_DIM), q.dtype)], axis=0)
            s = _dot_t(q, kh.astype(BF16)) * (C_HEAD_DIM ** -0.5)
            m = jnp.max(s, axis=-1, keepdims=True)
            e = jnp.exp(s - m)
            pr = (e / jnp.sum(e, axis=-1, keepdims=True)).astype(BF16)
            o = _dot(pr, vh.astype(BF16))
            o_ref[b, :, cols] = o[:tq].astype(o_ref.dtype)


def _mem_attn(q, mk, mv, *, interleaved, tq=512, bs=1):
    b, t, w = q.shape
    tq = _tile(t, tq)
    bs = math.gcd(bs, b)
    kv_spec = pl.BlockSpec((bs,) + mk.shape[1:], lambda i, j: (i, 0, 0))
    return pl.pallas_call(
        functools.partial(_mem_attn_kernel, n_heads=w // C_HEAD_DIM, interleaved=interleaved),
        out_shape=jax.ShapeDtypeStruct((b, t, w), BF16),
        grid=(b // bs, t // tq),
        in_specs=[pl.BlockSpec((bs, tq, w), lambda i, j: (i, j, 0)), kv_spec, kv_spec],
        out_specs=pl.BlockSpec((bs, tq, w), lambda i, j: (i, j, 0)),
        compiler_params=_params("parallel", "arbitrary"),
        name="mem_attn",
    )(q, mk, mv)


def _merge_kernel(oa_ref, obp_ref, obs_ref, ocp_ref, ocs_ref, ga_ref, gb_ref, gc_ref, wa_ref, wb_ref, wc_ref, m_ref,
                  *, npt):
    m = ga_ref[...].astype(F32) * _dot(oa_ref[...], wa_ref[...])
    m = m + gb_ref[...].astype(F32) * _dot(_group_rows(obp_ref, obs_ref, npt), wb_ref[...])
    m = m + gc_ref[...].astype(F32) * _dot(_group_rows(ocp_ref, ocs_ref, npt), wc_ref[...])
    m_ref[...] = m.astype(m_ref.dtype)


def _merge(oa, ob_p, ob_s, oc_p, oc_s, gates, wa, wb, wc, *, tm=1024, tn=512):
    t = oa.shape[0]
    d = wa.shape[1]
    tp = ob_p.shape[0]
    tm, tn = _tile(math.gcd(tp, ob_s.shape[0]), tm), _tile(d, tn)
    npt = tp // tm
    nj = d // tn
    gate = lambda b: pl.BlockSpec((tm, tn), lambda i, j: (i, b * nj + j))
    wcol = lambda width: pl.BlockSpec((width, tn), lambda i, j: (0, j))
    return pl.pallas_call(
        functools.partial(_merge_kernel, npt=npt),
        out_shape=jax.ShapeDtypeStruct((t, d), BF16),
        grid=(t // tm, nj),
        in_specs=[pl.BlockSpec((tm, oa.shape[1]), lambda i, j: (i, 0)),
                  *_group_specs(tm, ob_p.shape[1], npt), *_group_specs(tm, oc_p.shape[1], npt),
                  gate(0), gate(1), gate(2), wcol(wa.shape[0]), wcol(wb.shape[0]), wcol(wc.shape[0])],
        out_specs=pl.BlockSpec((tm, tn), lambda i, j: (i, j)),
        compiler_params=_params("parallel", "arbitrary"),
        name="merge",
    )(oa, ob_p, ob_s, oc_p, oc_s, gates, gates, gates, wa, wb, wc)


def _layer_norm(x, g, b):
    mu = jnp.mean(x, axis=-1, keepdims=True)
    var = jnp.mean(jnp.square(x - mu), axis=-1, keepdims=True)
    return (x - mu) * lax.rsqrt(var + LN_EPS) * g + b


def _pack_pairs(x):
    c = x.shape[1] // 2
    bits = lax.bitcast_convert_type(x.astype(BF16).astype(F32), jnp.uint32)
    return bits[:, c:] | (bits[:, :c] >> 16)


def _unpack_pairs(w):
    lo = lax.bitcast_convert_type(w << 16, F32)
    hi = lax.bitcast_convert_type(w & jnp.uint32(0xFFFF0000), F32)
    return jnp.concatenate([lo, hi], axis=1)


def _group_specs(tm, width, npt):
    return (pl.BlockSpec((tm, width), lambda i, *_: (jnp.minimum(i, npt - 1), 0)),
            pl.BlockSpec((tm, width), lambda i, *_: (jnp.maximum(i - npt, 0), 0)))


def _group_rows(p_ref, s_ref, npt):
    return jnp.where(pl.program_id(0) < npt, p_ref[...], s_ref[...])


def _out_kernel(m_ref, w_ref, xp_ref, xs_ref, g_ref, b_ref, h_ref, hb_ref, hp_ref, *, alpha, npt):
    y = alpha * _group_rows(xp_ref, xs_ref, npt) + _dot(m_ref[...], w_ref[...])
    h = _layer_norm(y, g_ref[...], b_ref[...])
    h_ref[...] = h
    hb_ref[...] = h.astype(BF16)
    hp_ref[...] = _pack_pairs(h)


def _out_proj(m, w_o, x_p, x_s, g, b, alpha, *, tm=512):
    t, d = m.shape
    tp = x_p.shape[0]
    tm = _tile(math.gcd(tp, x_s.shape[0]), tm)
    npt = tp // tm
    row = pl.BlockSpec((tm, d), lambda i: (i, 0))
    half = pl.BlockSpec((tm, d // 2), lambda i: (i, 0))
    vec = pl.BlockSpec((1, d), lambda i: (0, 0))
    return pl.pallas_call(
        functools.partial(_out_kernel, alpha=alpha, npt=npt),
        out_shape=(jax.ShapeDtypeStruct((t, d), F32), jax.ShapeDtypeStruct((t, d), BF16),
                   jax.ShapeDtypeStruct((t, d // 2), jnp.uint32)),
        grid=(t // tm,),
        in_specs=[row, pl.BlockSpec((d, d), lambda i: (0, 0)), *_group_specs(tm, d, npt), vec, vec],
        out_specs=(row, row, half),
        compiler_params=_params("parallel"),
        name="out_proj_ln1",
    )(m, w_o, x_p, x_s, g, b)


def _first_argmax(vals, idx, sentinel):
    m = jnp.max(vals, axis=0, keepdims=True)
    first = jnp.min(jnp.where(vals == m, idx, sentinel), axis=0, keepdims=True)
    return m, first


def _router_kernel(x_ref, w_ref, bias_ref, idx_ref, wt_ref, rank_ref, cnt_ref, run_scr):
    @pl.when(pl.program_id(0) == 0)
    def _():
        run_scr[...] = jnp.zeros_like(run_scr)

    s = jax.nn.sigmoid(_dot(x_ref[...], w_ref[...]))
    st = s.T
    sb = st + bias_ref[...]
    n_exp, tm = st.shape
    gs = n_exp // N_GROUPS
    neg = -jnp.inf
    ridx = lax.broadcasted_iota(jnp.int32, (gs, tm), 0)
    grp = []
    for g in range(N_GROUPS):
        blk = sb[g * gs:(g + 1) * gs]
        m1, first = _first_argmax(blk, ridx, gs)
        m2 = jnp.max(jnp.where(ridx == first, neg, blk), axis=0, keepdims=True)
        grp.append(m1 + m2)
    work = jnp.concatenate(grp, axis=0)
    gidx = lax.broadcasted_iota(jnp.int32, work.shape, 0)
    gmask = jnp.zeros(work.shape, jnp.bool_)
    for _ in range(TOPK_GROUPS):
        _, first = _first_argmax(work, gidx, N_GROUPS)
        sel = gidx == first
        gmask = jnp.logical_or(gmask, sel)
        work = jnp.where(sel, neg, work)
    masked = jnp.concatenate(
        [jnp.where(gmask[g:g + 1], sb[g * gs:(g + 1) * gs], neg) for g in range(N_GROUPS)], axis=0)
    eidx = lax.broadcasted_iota(jnp.int32, masked.shape, 0)
    ids, wts, sels = [], [], []
    for _ in range(TOP_K):
        _, first = _first_argmax(masked, eidx, n_exp)
        sel = eidx == first
        ids.append(first)
        sels.append(sel)
        wts.append(jnp.sum(jnp.where(sel, st, 0.0), axis=0, keepdims=True))
        masked = jnp.where(sel, neg, masked)
    wt = jnp.concatenate(wts, axis=0)
    idx_ref[...] = jnp.concatenate(ids, axis=0)
    wt_ref[...] = wt / jnp.sum(wt, axis=0, keepdims=True) * ROUTED_SCALE
    chosen = sels[0].astype(F32)
    for sel in sels[1:]:
        chosen = chosen + sel.astype(F32)
    earlier = lax.broadcasted_iota(jnp.int32, (tm, tm), 0) < lax.broadcasted_iota(jnp.int32, (tm, tm), 1)
    before = _dot(chosen.astype(BF16), earlier.astype(BF16)) + run_scr[...]
    rank_ref[...] = jnp.concatenate(
        [jnp.sum(jnp.where(sel, before, 0.0), axis=0, keepdims=True) for sel in sels], axis=0).astype(jnp.int32)
    run_scr[...] = run_scr[...] + jnp.sum(chosen, axis=1, keepdims=True)
    cnt_ref[...] = run_scr[...].astype(jnp.int32)


def _router(hb, w_router, bias, *, tm=256):
    t, d = hb.shape
    n_exp = w_router.shape[1]
    tm = _tile(t, tm)
    slot = pl.BlockSpec((TOP_K, tm), lambda i: (0, i))
    return pl.pallas_call(
        _router_kernel,
        out_shape=(jax.ShapeDtypeStruct((TOP_K, t), jnp.int32), jax.ShapeDtypeStruct((TOP_K, t), F32),
                   jax.ShapeDtypeStruct((TOP_K, t), jnp.int32), jax.ShapeDtypeStruct((n_exp, 1), jnp.int32)),
        grid=(t // tm,),
        in_specs=[pl.BlockSpec((tm, d), lambda i: (i, 0)),
                  pl.BlockSpec((d, n_exp), lambda i: (0, 0)),
                  pl.BlockSpec((n_exp, 1), lambda i: (0, 0))],
        out_specs=(slot, slot, slot, pl.BlockSpec((n_exp, 1), lambda i: (0, 0))),
        scratch_shapes=[pltpu.VMEM((n_exp, 1), F32)],
        compiler_params=_params("arbitrary"),
        name="router",
    )(hb, w_router, bias)


def _expert_kernel(b0_ref, nb_ref, nu_ref, x_hbm, wg_hbm, wu_hbm, wd_hbm, y_hbm, xbuf, ybuf, wbuf, sem_in, sem_out,
                   sem_w, w1_scr, wd_scr, *, f, up_off, n_blocks):
    e = pl.program_id(0)
    last = pl.num_programs(0) - 1
    b0 = b0_ref[e]
    nb = nb_ref[e]
    n_used = nu_ref[0]

    def x_copy(blk):
        slot = blk % X_SLOTS
        return pltpu.make_async_copy(x_hbm.at[pl.ds(blk * MOE_ROWS, MOE_ROWS)], xbuf.at[slot], sem_in.at[slot])

    def y_copy(blk):
        slot = blk % Y_SLOTS
        return pltpu.make_async_copy(ybuf.at[slot], y_hbm.at[pl.ds(blk * MOE_ROWS, MOE_ROWS)], sem_out.at[slot])

    def w_copies(ex):
        slot = ex % 2
        return [pltpu.make_async_copy(src.at[ex], wbuf.at[slot, j], sem_w.at[slot, j])
                for j, src in enumerate((wg_hbm, wu_hbm, wd_hbm))]

    @pl.when(e == 0)
    def _():
        for cp in w_copies(0):
            cp.start(priority=1)
        w1_scr[f:up_off, :] = jnp.zeros((up_off - f, w1_scr.shape[1]), BF16)

        for blk in range(GROUP):
            @pl.when(blk < n_used)
            def _():
                x_copy(blk).start()

    for cp in w_copies(e):
        cp.wait()

    @pl.when(e < last)
    def _():
        for cp in w_copies(e + 1):
            cp.start(priority=1)

    def process(blocks):
        for g in blocks:
            x_copy(g).wait()
        for g in blocks:
            @pl.when(g + GROUP < n_used)
            def _():
                x_copy(g + GROUP).start()

            @pl.when(g >= Y_SLOTS)
            def _():
                y_copy(g - Y_SLOTS).wait()

        hs = [_dot_t(_unpack_pairs(xbuf[g % X_SLOTS]).astype(BF16), w1_scr[...]) for g in blocks]
        acts = [jax.nn.silu(h[:, :f]) * h[:, up_off:up_off + f] for h in hs]
        for g, a in zip(blocks, acts):
            ybuf[g % Y_SLOTS] = _pack_pairs(_dot(a.astype(BF16), wd_scr[...]))
            y_copy(g).start()

    @pl.when(nb > 0)
    def _():
        wslot = e % 2
        w1_scr[0:f, :] = wbuf[wslot, 0].astype(BF16)
        w1_scr[up_off:up_off + f, :] = wbuf[wslot, 1].astype(BF16)
        wd_scr[...] = wbuf[wslot, 2].astype(BF16)

        def group(c, carry):
            g = b0 + GROUP * c
            process([g + j for j in range(GROUP)])
            return carry

        lax.fori_loop(0, nb // GROUP, group, 0)
        for rem in range(1, GROUP):
            @pl.when(nb % GROUP == rem)
            def _():
                process([b0 + nb - rem + j for j in range(rem)])

    @pl.when(e == last)
    def _():
        for back in range(Y_SLOTS, 0, -1):
            @pl.when(n_used >= back)
            def _():
                y_copy(n_used - back).wait()

        ybuf[0] = jnp.zeros(ybuf.shape[1:], ybuf.dtype)

        def fill(blk, carry):
            cp = pltpu.make_async_copy(ybuf.at[0], y_hbm.at[pl.ds(blk * MOE_ROWS, MOE_ROWS)], sem_out.at[0])
            cp.start()
            cp.wait()
            return carry

        lax.fori_loop(n_used, n_blocks, fill, 0)


def _experts(blk0, nblk, n_used, x_sorted, w_gate_t, w_up_t, w_down):
    n_rows = x_sorted.shape[0]
    n_exp, f, d = w_down.shape
    dp = d // 2
    up_off = -(-f // LANES) * LANES
    n_blocks = n_rows // MOE_ROWS
    hbm = pl.BlockSpec(memory_space=pl.ANY)
    grid_spec = pltpu.PrefetchScalarGridSpec(
        num_scalar_prefetch=3,
        grid=(n_exp,),
        in_specs=[hbm, hbm, hbm, hbm],
        out_specs=hbm,
        scratch_shapes=[pltpu.VMEM((X_SLOTS, MOE_ROWS, dp), jnp.uint32),
                        pltpu.VMEM((Y_SLOTS, MOE_ROWS, dp), jnp.uint32),
                        pltpu.VMEM((2, 3, f, d), F32),
                        pltpu.SemaphoreType.DMA((X_SLOTS,)), pltpu.SemaphoreType.DMA((Y_SLOTS,)),
                        pltpu.SemaphoreType.DMA((2, 3)),
                        pltpu.VMEM((up_off + f, d), BF16), pltpu.VMEM((f, d), BF16)],
    )
    return pl.pallas_call(
        functools.partial(_expert_kernel, f=f, up_off=up_off, n_blocks=n_blocks),
        out_shape=jax.ShapeDtypeStruct((n_rows, dp), jnp.uint32),
        grid_spec=grid_spec,
        compiler_params=_params("arbitrary"),
        name="experts",
    )(blk0, nblk, n_used, x_sorted, w_gate_t, w_up_t, w_down)


def _final_kernel(h_ref, hb_ref, yg_ref, wt_ref, wg_ref, wu_ref, wd_ref, g_ref, b_ref, op_ref, os_ref, *, alpha,
                  npt):
    hb = hb_ref[...]
    a = jax.nn.silu(_dot_t(hb, wg_ref[...])) * _dot_t(hb, wu_ref[...])
    shared = _dot(a.astype(BF16), wd_ref[...])
    wt = wt_ref[...]
    routed = wt[:, 0:1] * _unpack_pairs(yg_ref[0])
    for k in range(1, yg_ref.shape[0]):
        routed = routed + wt[:, k:k + 1] * _unpack_pairs(yg_ref[k])
    y = _layer_norm(alpha * h_ref[...] + (routed + shared), g_ref[...], b_ref[...])

    @pl.when(pl.program_id(0) < npt)
    def _():
        op_ref[...] = y

    @pl.when(pl.program_id(0) >= npt)
    def _():
        os_ref[...] = y


def _final(h, hb, yg, wt, ws_gate_t, ws_up_t, ws_down, g, b, alpha, tp, *, tm=256):
    t, d = h.shape
    f = ws_down.shape[0]
    tm = _tile(math.gcd(tp, t - tp), tm)
    npt = tp // tm
    row = pl.BlockSpec((tm, d), lambda i: (i, 0))
    vec = pl.BlockSpec((1, d), lambda i: (0, 0))
    wfull = pl.BlockSpec((f, d), lambda i: (0, 0))
    return pl.pallas_call(
        functools.partial(_final_kernel, alpha=alpha, npt=npt),
        out_shape=(jax.ShapeDtypeStruct((tp, d), F32), jax.ShapeDtypeStruct((t - tp, d), F32)),
        grid=(t // tm,),
        in_specs=[row, row, pl.BlockSpec((yg.shape[0], tm, d // 2), lambda i: (0, i, 0)),
                  pl.BlockSpec((tm, wt.shape[1]), lambda i: (i, 0)), wfull, wfull, wfull, vec, vec],
        out_specs=_group_specs(tm, d, npt),
        compiler_params=_params("arbitrary"),
        name="shared_combine_ln2",
    )(h, hb, yg, wt, ws_gate_t, ws_up_t, ws_down, g, b)


def _dispatch(idx_t, rank_t, counts):
    top_k, t = idx_t.shape
    n_exp = counts.shape[0]
    tk = t * top_k
    pcounts = (counts + MOE_ROWS - 1) // MOE_ROWS * MOE_ROWS
    pend = jnp.cumsum(pcounts)
    pstart = pend - pcounts
    onehot = idx_t[:, :, None] == jnp.arange(n_exp, dtype=jnp.int32)
    dest = rank_t + jnp.sum(jnp.where(onehot, pstart, 0), axis=-1)
    n_blocks = -(-tk // MOE_ROWS) + n_exp
    n_used = (pend[-1:] // MOE_ROWS).astype(jnp.int32)
    n_rows = n_blocks * MOE_ROWS
    assert n_rows == tk + n_exp * MOE_ROWS
    i = jnp.arange(MOE_ROWS, dtype=jnp.int32)[None, :]
    pad_keys = jnp.where(i < (pcounts - counts)[:, None], (pstart + counts)[:, None] + i, n_rows)
    keys = jnp.concatenate([dest.reshape(-1), pad_keys.reshape(-1)]).astype(jnp.int32)
    toks = jnp.concatenate([jnp.tile(jnp.arange(t, dtype=jnp.int32), top_k),
                            jnp.arange(n_exp * MOE_ROWS, dtype=jnp.int32) % t])
    _, row_tok = lax.sort((keys, toks), num_keys=1)
    return dest, row_tok, (pstart // MOE_ROWS).astype(jnp.int32), (pcounts // MOE_ROWS).astype(jnp.int32), n_used


def _moe(h, hb, hp, w_router, router_bias, w_gate_t, w_up_t, w_down, ws_gate_t, ws_up_t, ws_down, g, b, alpha, tp):
    n_exp = w_router.shape[1]
    idx_t, wt_t, rank_t, counts = _router(hb, w_router, router_bias.reshape(n_exp, 1))
    dest, row_tok, blk0, nblk, n_used = _dispatch(idx_t, rank_t, counts.reshape(n_exp))
    x_sorted = hp[row_tok]
    y_sorted = _experts(blk0, nblk, n_used, x_sorted, w_gate_t, w_up_t, w_down)
    yg = y_sorted[dest]
    return _final(h, hb, yg, wt_t.T, ws_gate_t, ws_up_t, ws_down, g, b, alpha, tp)


def _rope_tables(pos):
    half = B_HEAD_DIM // 2
    inv = 1.0 / (ROPE_THETA ** (jnp.arange(half, dtype=F32) / half))
    ang = pos.astype(F32)[:, None] * inv[None, :]
    cos, sin = jnp.cos(ang), jnp.sin(ang)
    reps = LANES // B_HEAD_DIM
    return (jnp.concatenate([cos, cos] * reps, axis=1), jnp.concatenate([-sin, sin] * reps, axis=1))


def kernel(x_prompt, x_sample, mem_prompt, cache_k, cache_v, page_table, cache_mem_k, cache_mem_v, w_in, a_ln_g, a_ln_b, a_ws, a_bs, lam_q1, lam_k1, lam_q2, lam_k2, b_subln_g, w_mk, w_mv, w_pa, w_pb, w_pc, w_o, ln1_g, ln1_b, w_router, router_bias, w_gate, w_up, w_down, ws_gate, ws_up, ws_down, ln2_g, ln2_b):
    depth = w_in.shape[0]
    assert depth == 1, "single-layer step"
    layer = 0
    batch, seq, d = x_prompt.shape
    n_seq, dec, _ = x_sample.shape
    n_mem = mem_prompt.shape[1]
    n_phys, page, n_bh = cache_k.shape[1:4]
    past = page_table.shape[1] * page
    c_width = cache_mem_k.shape[3] * C_HEAD_DIM
    a_width = A_GROUPS * A_GROUP_DIM
    qk_width = n_bh * 2 * B_HEAD_DIM
    bv_width = n_bh * B_V_DIM
    assert CHUNK % dec == 0 and seq % CHUNK == 0
    alpha = (2.0 * depth) ** 0.25
    lam_init = _lambda_init(layer)
    tp, ts = batch * seq, n_seq * dec
    t = tp + ts

    x_p, x_s = x_prompt.reshape(tp, d), x_sample.reshape(ts, d)
    xb = jnp.concatenate([x_p.astype(BF16), x_s.astype(BF16)], axis=0)
    w_in_b = w_in[layer].astype(BF16)
    o = [0, 2 * a_width]
    for wdt in (qk_width, qk_width, bv_width, c_width, 3 * d):
        o.append(o[-1] + wdt)
    pos = jnp.concatenate([jnp.tile(jnp.arange(seq), batch), past + jnp.tile(jnp.arange(dec), n_seq)])
    rope = _rope_tables(pos)

    reps = CHUNK // dec
    ws = a_ws[layer]
    ws_s = jnp.einsum("ab,gts->gatbs", jnp.eye(reps, dtype=ws.dtype), ws[:, :dec, :dec]).reshape(
        A_GROUPS, CHUNK, CHUNK)
    ws2 = jnp.stack([ws, ws_s]).astype(BF16)
    bs = a_bs[layer]
    bs2 = jnp.stack([bs, jnp.tile(bs[:, :dec], (1, reps))])
    bs2 = jnp.broadcast_to(bs2[..., None], bs2.shape + (CHUNK,)).astype(F32)

    o_a, vn = _proj_a(xb, w_in_b[:, o[0]:o[1]], a_ln_g[layer].reshape(1, a_width),
                      a_ln_b[layer].reshape(1, a_width), ws2, bs2, tp)
    q = _mm(xb, w_in_b[:, o[1]:o[2]], BF16, epilogue="rope", scale=B_HEAD_DIM ** -0.5, rope=rope)
    v_p = _mm(xb, w_in_b[:, o[3]:o[4]], F32, m=tp)
    v_s = _mm(xb[tp:], w_in_b[:, o[3]:o[4]], F32)
    cq = _mm(xb, w_in_b[:, o[4]:o[5]], BF16)
    gates = _mm(xb, w_in_b[:, o[5]:o[6]], BF16, epilogue="sigmoid")
    w_k = w_in_b[:, o[2]:o[3]]
    w_k_t = w_k.T
    cos_t, sin_t = rope[0][:, :B_HEAD_DIM // 2].T, rope[1][:, B_HEAD_DIM // 2:B_HEAD_DIM].T
    kt_p = _proj_kt(w_k_t, xb, cos_t, sin_t, batch, tp, tm=512)
    xs_t = jnp.swapaxes(x_sample, 0, 1).reshape(ts, d).astype(BF16)
    pos_s = past + jnp.repeat(jnp.arange(dec), n_seq)
    cos_s, sin_s = _rope_tables(pos_s)
    kt_s = _proj_kt(w_k_t, xs_t, cos_s[:, :B_HEAD_DIM // 2].T, sin_s[:, B_HEAD_DIM // 2:B_HEAD_DIM].T, dec,
                    ts, tm=n_seq)
    k_s = _mm(xb[tp:], w_k, F32, epilogue="rope", rope=(rope[0][tp:], rope[1][tp:]))

    lamv = jnp.stack([lam_q1[layer], lam_k1[layer], lam_q2[layer], lam_k2[layer]]).astype(F32)
    g_sub = b_subln_g[layer].reshape(1, B_V_DIM)
    ob_p = _attn_prompt(lamv, q, kt_p, v_p, g_sub, batch, seq, n_bh, lam_init)
    cache_kt = jnp.transpose(cache_k[layer], (0, 2, 3, 4, 1)).reshape(n_phys, n_bh, 2 * B_HEAD_DIM, page)
    ob_s = _attn_sample(page_table, lamv, q[tp:].reshape(n_seq, dec, qk_width),
                        k_s.reshape(n_seq, dec, qk_width), v_s.reshape(n_seq, dec, bv_width), g_sub,
                        cache_kt, cache_v[layer].reshape(n_phys, page * n_bh, B_V_DIM), lam_init)

    c_heads = c_width // C_HEAD_DIM
    memb = mem_prompt.reshape(batch * n_mem, d).astype(BF16)
    mk = _mm(memb, w_mk[layer].astype(BF16), F32)
    mv = _mm(memb, w_mv[layer].astype(BF16), F32)
    oc_p = _mem_attn(cq[:tp].reshape(batch, seq, c_width), mk.reshape(batch, n_mem, c_width),
                     mv.reshape(batch, n_mem, c_width), interleaved=False)
    oc_s = _mem_attn(cq[tp:].reshape(n_seq, dec, c_width),
                     cache_mem_k[layer].reshape(n_seq, n_mem * c_heads, C_HEAD_DIM),
                     cache_mem_v[layer].reshape(n_seq, n_mem * c_heads, C_HEAD_DIM), interleaved=True, bs=8)

    m = _merge(o_a, ob_p, ob_s.reshape(ts, bv_width), oc_p.reshape(tp, c_width), oc_s.reshape(ts, c_width), gates,
               w_pa[layer].astype(BF16), w_pb[layer].astype(BF16), w_pc[layer].astype(BF16))
    h1, h1b, h1p = _out_proj(m, w_o[layer].astype(BF16), x_p, x_s, ln1_g[layer].reshape(1, d),
                             ln1_b[layer].reshape(1, d), alpha)
    y_p, y_s = _moe(h1, h1b, h1p, w_router[layer].astype(BF16), router_bias[layer],
                    jnp.swapaxes(w_gate[layer], 1, 2), jnp.swapaxes(w_up[layer], 1, 2), w_down[layer],
                    ws_gate[layer].T.astype(BF16), ws_up[layer].T.astype(BF16), ws_down[layer].astype(BF16),
                    ln2_g[layer].reshape(1, d), ln2_b[layer].reshape(1, d), alpha, tp)

    k_prompt = jnp.transpose(kt_p.reshape(batch, n_bh, 2, B_HEAD_DIM, seq), (0, 4, 1, 2, 3))
    k_sample = jnp.transpose(kt_s.reshape(dec, n_bh, 2, B_HEAD_DIM, n_seq), (4, 0, 1, 2, 3))
    return (y_p.reshape(batch, seq, d),
            y_s.reshape(n_seq, dec, d),
            k_prompt[None],
            v_p.reshape(1, batch, seq, n_bh, B_V_DIM),
            mk.reshape(1, batch, n_mem, c_heads, C_HEAD_DIM),
            mv.reshape(1, batch, n_mem, c_heads, C_HEAD_DIM),
            k_sample[None],
            v_s.reshape(1, n_seq, dec, n_bh, B_V_DIM),
            vn[tp:].reshape(1, n_seq, dec, a_width))
```
